```python
import math
import jax, jax.numpy as jnp
from jax import lax
import numpy as np

D_MODEL = 1024
BATCH = 2
SEQ = 8192
DEPTH = 2

HEAD_DIM = 64
DIFF_HEADS = 4
FOX_HEADS = 8
DIFF_QK_WIDTH = 2 * DIFF_HEADS * HEAD_DIM
DIFF_V_WIDTH = DIFF_HEADS * 2 * HEAD_DIM
FOX_WIDTH = FOX_HEADS * HEAD_DIM
ROT_DIM = HEAD_DIM // 4
ROPE_THETA = 500000.0
Q_BLOCK = 128
N_GROUPS = 4
EXPERTS_PER_GROUP = 8
EXPERT_FF = 512
TOP_K_INNER = 2
EPS = 1e-6
SPLIT_SIZES = (DIFF_QK_WIDTH, DIFF_QK_WIDTH, DIFF_V_WIDTH, FOX_WIDTH, FOX_WIDTH, FOX_WIDTH, FOX_HEADS, 2 * D_MODEL)
IN_COLS = sum(SPLIT_SIZES)

kernel_name = "hybrid_diffattn_fox_hmoe_block"


def rms_norm(x, g):
    xf = x.astype(jnp.float32)
    y = xf * lax.rsqrt(jnp.mean(xf * xf, axis=-1, keepdims=True) + EPS)
    return (y * g.astype(jnp.float32)).astype(x.dtype)


def split_heads(t, n_heads):
    b, s, w = t.shape
    return t.reshape(b, s, n_heads, w // n_heads).transpose(0, 2, 1, 3)


def merge_heads(t):
    b, h, s, d = t.shape
    return t.transpose(0, 2, 1, 3).reshape(b, s, h * d)


def rotary_tables(positions, dtype):
    inv_freq = ROPE_THETA ** (-jnp.arange(0, ROT_DIM, 2, dtype=jnp.float32) / ROT_DIM)
    ang = positions.astype(jnp.float32)[..., None] * inv_freq
    return jnp.cos(ang)[:, None].astype(dtype), jnp.sin(ang)[:, None].astype(dtype)


def partial_rotary(t, cos, sin):
    half = ROT_DIM // 2
    t1, t2, rest = t[..., :half], t[..., half:ROT_DIM], t[..., ROT_DIM:]
    return jnp.concatenate([t1 * cos - t2 * sin, t2 * cos + t1 * sin, rest], axis=-1)


def causal_block_attention(q, k, v, q_bias=None, k_bias=None):
    b, h, s, dk = q.shape
    dv = v.shape[-1]
    scale = dk ** -0.5
    k_pos = jnp.arange(s)

    def one_block(i):
        start = i * Q_BLOCK
        qb = lax.dynamic_slice_in_dim(q, start, Q_BLOCK, axis=2)
        logits = jnp.einsum('bhqd,bhkd->bhqk', qb, k).astype(jnp.float32) * scale
        if q_bias is not None:
            qbb = lax.dynamic_slice_in_dim(q_bias, start, Q_BLOCK, axis=2)
            logits = logits + qbb[..., :, None] + k_bias[..., None, :]
        q_pos = start + jnp.arange(Q_BLOCK)
        mask = k_pos[None, :] <= q_pos[:, None]
        logits = jnp.where(mask, logits, -jnp.inf)
        p = jax.nn.softmax(logits, axis=-1).astype(v.dtype)
        return jnp.einsum('bhqk,bhkv->bhqv', p, v)

    out = lax.map(one_block, jnp.arange(s // Q_BLOCK))
    return out.transpose(1, 2, 0, 3, 4).reshape(b, h, s, dv)


def hierarchical_moe(h, w_rg, b_rg, w_re, b_re, w_gate, w_up, w_down):
    t = h.shape[0]
    p_group = jax.nn.softmax((h @ w_rg + b_rg).astype(jnp.float32), axis=-1)
    p_g_top, g_idx = lax.top_k(p_group, 1)
    lg_exp = (h @ w_re + b_re).astype(jnp.float32).reshape(t, N_GROUPS, EXPERTS_PER_GROUP)
    lg_sel = jnp.take_along_axis(lg_exp, g_idx[:, :, None], axis=1)[:, 0]
    p_exp = jax.nn.softmax(lg_sel, axis=-1)
    p_e_top, e_idx = lax.top_k(p_exp, TOP_K_INNER)
    p_e_top = p_e_top / jnp.sum(p_e_top, axis=-1, keepdims=True)
    w_inner = jnp.sum(jax.nn.one_hot(e_idx, EXPERTS_PER_GROUP, dtype=jnp.float32) * p_e_top[..., None], axis=1)
    gate = (jax.nn.one_hot(g_idx[:, 0], N_GROUPS, dtype=jnp.float32)[:, :, None]
            * (p_g_top[:, 0, None, None] * w_inner[:, None, :])).astype(h.dtype)
    out = jnp.zeros_like(h)
    for g in range(N_GROUPS):
        a = jnp.einsum('td,edf->tef', h, w_gate[g])
        u = jnp.einsum('td,edf->tef', h, w_up[g])
        hid = jax.nn.silu(a) * u * gate[:, g, :, None]
        out = out + jnp.einsum('tef,efd->td', hid, w_down[g])
    return out


def setup_inputs(seed: int = 0) -> dict:
    key = jax.random.key(seed)
    ks = jax.random.split(key, 26)
    f32 = jnp.float32
    L, D, d = DEPTH, D_MODEL, HEAD_DIM
    G, E, F = N_GROUPS, EXPERTS_PER_GROUP, EXPERT_FF

    def nrm(k, shape, scale):
        return jax.random.normal(k, shape, f32) * scale

    def gain(k, shape):
        return 1.0 + 0.05 * jax.random.normal(k, shape, f32)

    return {
        "x": jax.random.normal(ks[0], (BATCH, SEQ, D), f32),
        "positions": jnp.broadcast_to(jnp.arange(SEQ, dtype=jnp.int32), (BATCH, SEQ)),
        "ln1_g": gain(ks[1], (L, D)),
        "w_in": nrm(ks[2], (L, D, IN_COLS), D ** -0.5),
        "b_forget": 2.0 + 0.5 * jax.random.normal(ks[3], (L, FOX_HEADS), f32),
        "q_norm_diff_g": gain(ks[4], (L, d)),
        "k_norm_diff_g": gain(ks[5], (L, d)),
        "lam_q1": nrm(ks[6], (L, d), 0.1),
        "lam_k1": nrm(ks[7], (L, d), 0.1),
        "lam_q2": nrm(ks[8], (L, d), 0.1),
        "lam_k2": nrm(ks[9], (L, d), 0.1),
        "subln_g": gain(ks[10], (L, 2 * d)),
        "q_norm_fox_g": gain(ks[11], (L, d)),
        "k_norm_fox_g": gain(ks[12], (L, d)),
        "w_diff_out": nrm(ks[13], (L, DIFF_V_WIDTH, D), DIFF_V_WIDTH ** -0.5),
        "w_fox_out": nrm(ks[14], (L, FOX_WIDTH, D), FOX_WIDTH ** -0.5),
        "w_o": nrm(ks[15], (L, D, D), D ** -0.5),
        "ln2_g": gain(ks[16], (L, D)),
        "w_router_group": nrm(ks[17], (L, D, G), D ** -0.5),
        "b_router_group": nrm(ks[18], (L, G), 0.01),
        "w_router_expert": nrm(ks[19], (L, D, G * E), D ** -0.5),
        "b_router_expert": nrm(ks[20], (L, G * E), 0.01),
        "w_exp_gate": nrm(ks[21], (L, G, E, D, F), D ** -0.5),
        "w_exp_up": nrm(ks[22], (L, G, E, D, F), D ** -0.5),
        "w_exp_down": nrm(ks[23], (L, G, E, F, D), F ** -0.5),
    }


def reference(x, positions, ln1_g, w_in, b_forget, q_norm_diff_g, k_norm_diff_g, lam_q1, lam_k1, lam_q2, lam_k2,
              subln_g, q_norm_fox_g, k_norm_fox_g, w_diff_out, w_fox_out, w_o, ln2_g, w_router_group,
              b_router_group, w_router_expert, b_router_expert, w_exp_gate, w_exp_up, w_exp_down):
    b, s, dm = x.shape
    cos, sin = rotary_tables(positions, x.dtype)
    split_points = [int(v) for v in np.cumsum(SPLIT_SIZES)[:-1]]
    for l in range(DEPTH):
        lambda_init = 0.8 - 0.6 * math.exp(-0.3 * l)
        h = rms_norm(x, ln1_g[l])
        proj = h @ w_in[l]
        qa, ka, va, qf, kf, vf, fz, gz = jnp.split(proj, split_points, axis=-1)

        qa = partial_rotary(rms_norm(split_heads(qa, 2 * DIFF_HEADS), q_norm_diff_g[l]), cos, sin)
        ka = partial_rotary(rms_norm(split_heads(ka, 2 * DIFF_HEADS), k_norm_diff_g[l]), cos, sin)
        va = split_heads(va, DIFF_HEADS)
        oa = causal_block_attention(qa, ka, jnp.concatenate([va, va], axis=1))
        lam = (jnp.exp(jnp.sum(lam_q1[l].astype(jnp.float32) * lam_k1[l].astype(jnp.float32)))
               - jnp.exp(jnp.sum(lam_q2[l].astype(jnp.float32) * lam_k2[l].astype(jnp.float32)))
               + lambda_init)
        oa = oa[:, :DIFF_HEADS] - lam.astype(oa.dtype) * oa[:, DIFF_HEADS:]
        oa = rms_norm(oa, subln_g[l]) * (1.0 - lambda_init)
        ya = merge_heads(oa) @ w_diff_out[l]

        qf = rms_norm(split_heads(qf, FOX_HEADS), q_norm_fox_g[l])
        kf = rms_norm(split_heads(kf, FOX_HEADS), k_norm_fox_g[l])
        vf = split_heads(vf, FOX_HEADS)
        log_f = jax.nn.log_sigmoid(fz.astype(jnp.float32) + b_forget[l].astype(jnp.float32))
        cum = jnp.cumsum(log_f, axis=1).transpose(0, 2, 1)
        of = causal_block_attention(qf, kf, vf, cum, -cum)
        yf = merge_heads(of) @ w_fox_out[l]

        gates = jax.nn.sigmoid(gz.astype(jnp.float32)).astype(x.dtype)
        g_a, g_f = gates[..., :dm], gates[..., dm:]
        x = x + (g_a * ya + g_f * yf) @ w_o[l]

        h2 = rms_norm(x, ln2_g[l]).reshape(b * s, dm)
        y = hierarchical_moe(h2, w_router_group[l], b_router_group[l], w_router_expert[l], b_router_expert[l],
                             w_exp_gate[l], w_exp_up[l], w_exp_down[l])
        x = x + y.reshape(b, s, dm)
    return x
```

```python
import functools
import math

import jax
import jax.numpy as jnp
from jax import lax
from jax.experimental import pallas as pl
from jax.experimental.pallas import tpu as pltpu

F32 = jnp.float32
BF16 = jnp.bfloat16

HEAD_DIM = 64
DIFF_HEADS = 4
FOX_HEADS = 8
ROT_DIM = HEAD_DIM // 4
ROPE_THETA = 500000.0
N_GROUPS = 4
EXPERTS_PER_GROUP = 8
N_EXPERTS = N_GROUPS * EXPERTS_PER_GROUP
EPS = 1e-6
LANES = 128
NEG_BIG = -1e30

VMEM_LIMIT = 56 * 1024 * 1024


def _cparams(sem):
    return pltpu.CompilerParams(dimension_semantics=sem, vmem_limit_bytes=VMEM_LIMIT)


def _const_spec(shape):
    nd = len(shape)
    return pl.BlockSpec(shape, lambda *_: (0,) * nd)


def _inproj_kernel(x_ref, g1_ref, wqa, wka, wva, wqf, wkf, wvf, wfz, wgz, bd_ref, gqa, gka, gqf, gkf, bf_ref,
                   cos_ref, sina_ref, sinb_ref,
                   qa_o, ka_o, va_o, qf_o, kf_o, vf_o, cum_o, gate_o, carry_ref, *, tiles_per_seq, tm):
    i = pl.program_id(0)
    x = x_ref[...]
    ms = jnp.mean(x * x, axis=-1, keepdims=True)
    h = (x * lax.rsqrt(ms + EPS) * g1_ref[...]).astype(BF16)

    def proj(w_ref):
        return jnp.dot(h, w_ref[...], preferred_element_type=F32)

    def head_norm(y, g_ref):
        hms = jnp.dot((y * y).astype(BF16), bd_ref[...], preferred_element_type=F32)
        return y * lax.rsqrt(hms + EPS) * g_ref[...]

    reps = qa_o.shape[1] // LANES
    cos = jnp.concatenate([cos_ref[...]] * reps, axis=1)
    sina = jnp.concatenate([sina_ref[...]] * reps, axis=1)
    sinb = jnp.concatenate([sinb_ref[...]] * reps, axis=1)
    width = qa_o.shape[1]
    half = ROT_DIM // 2

    def rotary(y):
        return y * cos + pltpu.roll(y, width - half, 1) * sina + pltpu.roll(y, half, 1) * sinb

    scale = HEAD_DIM ** -0.5
    qa_o[...] = (rotary(head_norm(proj(wqa), gqa)) * scale).astype(BF16)
    ka_o[...] = rotary(head_norm(proj(wka), gka)).astype(BF16)
    va_o[...] = proj(wva).astype(BF16)
    qf_o[...] = (head_norm(proj(wqf), gqf) * scale).astype(BF16)
    kf_o[...] = head_norm(proj(wkf), gkf).astype(BF16)
    vf_o[...] = proj(wvf).astype(BF16)

    z = proj(wfz) + bf_ref[...]
    ls = jnp.minimum(z, 0.0) - jnp.log(1.0 + jnp.exp(-jnp.abs(z)))
    row = lax.broadcasted_iota(jnp.int32, (tm, tm), 0)
    col = lax.broadcasted_iota(jnp.int32, (tm, tm), 1)
    tri = jnp.where(row >= col, 1.0, 0.0).astype(BF16)
    p1 = ls.astype(BF16)
    r1 = ls - p1.astype(F32)
    p2 = r1.astype(BF16)
    p3 = (r1 - p2.astype(F32)).astype(BF16)
    csum = (jnp.dot(tri, p1, preferred_element_type=F32) + jnp.dot(tri, p2, preferred_element_type=F32)
            + jnp.dot(tri, p3, preferred_element_type=F32))

    @pl.when(i % tiles_per_seq == 0)
    def _():
        carry_ref[...] = jnp.zeros_like(carry_ref)

    cum = csum + carry_ref[...]
    cum_o[...] = cum
    carry_ref[...] = cum[tm - 1:tm, :]

    gw = gate_o.shape[1]
    step = 512
    for c in range(gw // step):
        zz = jnp.dot(h, wgz[:, c * step:(c + 1) * step], preferred_element_type=F32)
        gate_o[:, c * step:(c + 1) * step] = jax.nn.sigmoid(zz).astype(BF16)


def _inproj(x2d, g1, ws, bd, gqa, gka, gqf, gkf, bfz, cos, sina, sinb, seq, tm):
    t, d = x2d.shape
    wqa, wka, wva, wqf, wkf, wvf, wfz, wgz = ws
    grid = (t // tm,)
    row_spec = lambda w: pl.BlockSpec((tm, w), lambda i: (i, 0))
    in_specs = [row_spec(d), _const_spec(g1.shape)] + [_const_spec(w.shape) for w in ws] + [
        _const_spec(bd.shape), _const_spec(gqa.shape), _const_spec(gka.shape), _const_spec(gqf.shape),
        _const_spec(gkf.shape), _const_spec(bfz.shape), row_spec(LANES), row_spec(LANES), row_spec(LANES)]
    widths = [wqa.shape[1], wka.shape[1], wva.shape[1], wqf.shape[1], wkf.shape[1], wvf.shape[1]]
    out_shape = [jax.ShapeDtypeStruct((t, w), BF16) for w in widths] + [
        jax.ShapeDtypeStruct((t, LANES), F32), jax.ShapeDtypeStruct((t, wgz.shape[1]), BF16)]
    out_specs = [row_spec(w) for w in widths] + [row_spec(LANES), row_spec(wgz.shape[1])]
    return pl.pallas_call(
        functools.partial(_inproj_kernel, tiles_per_seq=seq // tm, tm=tm),
        grid=grid, in_specs=in_specs, out_specs=out_specs, out_shape=out_shape,
        scratch_shapes=[pltpu.VMEM((1, LANES), F32)],
        compiler_params=_cparams(("arbitrary",)), name="inproj",
    )(x2d, g1, *ws, bd, gqa, gka, gqf, gkf, bfz, cos, sina, sinb)


def _flash_update(q, k, v, carry, bias, masked):
    m, l, acc = carry
    s = lax.dot_general(q, k, (((1,), (1,)), ((), ())), preferred_element_type=F32)
    if bias is not None:
        s = s + bias
    if masked:
        row = lax.broadcasted_iota(jnp.int32, s.shape, 0)
        col = lax.broadcasted_iota(jnp.int32, s.shape, 1)
        s = jnp.where(col <= row, s, NEG_BIG)
    m_new = jnp.maximum(m, jnp.max(s, axis=-1, keepdims=True))
    alpha = jnp.exp(m - m_new)
    p = jnp.exp(s - m_new)
    l = alpha * l + jnp.sum(p, axis=-1, keepdims=True)
    acc = alpha * acc + jnp.dot(p.astype(BF16), v, preferred_element_type=F32)
    return m_new, l, acc


def _flash_init(tq, dv):
    return (jnp.full((tq, 1), NEG_BIG, F32), jnp.zeros((tq, 1), F32), jnp.zeros((tq, dv), F32))


def _diff_attn_kernel(q1_ref, q2_ref, k1_ref, k2_ref, v_ref, lq1, lk1, lq2, lk2, sg_ref, o_ref, *, tq, lambda_init):
    qi = pl.program_id(2)
    q1 = q1_ref[0, 0]
    q2 = q2_ref[0, 0]
    dv = v_ref.shape[2]

    def step(j, carry, masked):
        c1, c2 = carry
        ks = pl.ds(pl.multiple_of(j * tq, tq), tq)
        v = v_ref[0, ks, :]
        c1 = _flash_update(q1, k1_ref[0, 0, ks, :], v, c1, None, masked)
        c2 = _flash_update(q2, k2_ref[0, 0, ks, :], v, c2, None, masked)
        return c1, c2

    carry = (_flash_init(tq, dv), _flash_init(tq, dv))
    carry = lax.fori_loop(0, qi, lambda j, c: step(j, c, False), carry)
    (_, l1, a1), (_, l2, a2) = step(qi, carry, True)

    lam = (jnp.exp(jnp.sum(lq1[...] * lk1[...], axis=-1, keepdims=True))
           - jnp.exp(jnp.sum(lq2[...] * lk2[...], axis=-1, keepdims=True)) + lambda_init)
    o = a1 / l1 - lam * (a2 / l2)
    ms = jnp.mean(o * o, axis=-1, keepdims=True)
    o = o * lax.rsqrt(ms + EPS) * sg_ref[...] * (1.0 - lambda_init)
    o_ref[0] = o.astype(o_ref.dtype)


def _diff_attention(qh, kh, va, lq1, lk1, lq2, lk2, subg, lambda_init, tq):
    b, _, s, dk = qh.shape
    dv = 2 * HEAD_DIM
    grid = (b, DIFF_HEADS, s // tq)
    q_spec = lambda off: pl.BlockSpec((1, 1, tq, dk), lambda bi, hi, qi: (bi, hi + off, qi, 0))
    k_spec = lambda off: pl.BlockSpec((1, 1, s, dk), lambda bi, hi, qi: (bi, hi + off, 0, 0))
    vec = lambda a: pl.BlockSpec(a.shape, lambda bi, hi, qi: (0, 0))
    return pl.pallas_call(
        functools.partial(_diff_attn_kernel, tq=tq, lambda_init=lambda_init),
        grid=grid,
        in_specs=[q_spec(0), q_spec(DIFF_HEADS), k_spec(0), k_spec(DIFF_HEADS),
                  pl.BlockSpec((1, s, dv), lambda bi, hi, qi: (bi, 0, hi)),
                  vec(lq1), vec(lk1), vec(lq2), vec(lk2), vec(subg)],
        out_specs=pl.BlockSpec((1, tq, dv), lambda bi, hi, qi: (bi, qi, hi)),
        out_shape=jax.ShapeDtypeStruct((b, s, DIFF_HEADS * dv), BF16),
        compiler_params=_cparams(("arbitrary", "arbitrary", "arbitrary")), name="diff_attn",
    )(qh, qh, kh, kh, va, lq1, lk1, lq2, lk2, subg)


def _fox_attn_kernel(q_ref, k_ref, v_ref, ccol_ref, crow_ref, o_ref, *, tq):
    hp = pl.program_id(1)
    qi = pl.program_id(2)
    dv = v_ref.shape[3]
    outs = []
    for u in range(2):
        q = q_ref[0, u]
        lane = lax.broadcasted_iota(jnp.int32, ccol_ref.shape[1:], 1)
        ct = jnp.sum(jnp.where(lane == 2 * hp + u, ccol_ref[0], 0.0), axis=-1, keepdims=True)

        def step(j, carry, masked, u=u, q=q, ct=ct):
            ks = pl.ds(pl.multiple_of(j * tq, tq), tq)
            cs = crow_ref[0, pl.ds(2 * hp + u, 1), ks]
            return _flash_update(q, k_ref[0, u, ks, :], v_ref[0, u, ks, :], carry, ct - cs, masked)

        carry = lax.fori_loop(0, qi, lambda j, c: step(j, c, False), _flash_init(tq, dv))
        _, l, a = step(qi, carry, True)
        outs.append(a / l)
    o_ref[0] = jnp.concatenate(outs, axis=-1).astype(o_ref.dtype)


def _fox_attention(qh, kh, vh, cum_col, cum_row, tq):
    b, nh, s, dk = qh.shape
    grid = (b, nh // 2, s // tq)
    return pl.pallas_call(
        functools.partial(_fox_attn_kernel, tq=tq),
        grid=grid,
        in_specs=[pl.BlockSpec((1, 2, tq, dk), lambda bi, hp, qi: (bi, hp, qi, 0)),
                  pl.BlockSpec((1, 2, s, dk), lambda bi, hp, qi: (bi, hp, 0, 0)),
                  pl.BlockSpec((1, 2, s, dk), lambda bi, hp, qi: (bi, hp, 0, 0)),
                  pl.BlockSpec((1, tq, LANES), lambda bi, hp, qi: (bi, qi, 0)),
                  pl.BlockSpec((1, nh, s), lambda bi, hp, qi: (bi, 0, 0))],
        out_specs=pl.BlockSpec((1, tq, 2 * dk), lambda bi, hp, qi: (bi, qi, hp)),
        out_shape=jax.ShapeDtypeStruct((b, s, nh * dk), BF16),
        compiler_params=_cparams(("arbitrary", "arbitrary", "arbitrary")), name="fox_attn",
    )(qh, kh, vh, cum_col, cum_row)


def _post_kernel(x_ref, oa_ref, of_ref, gate_ref, wda, wfo, wo, g2_ref, wr_ref, br_ref, x_o, h2_o, route_o, *, tm):
    d = x_ref.shape[1]
    ya = jnp.dot(oa_ref[...], wda[...], preferred_element_type=F32)
    yf = jnp.dot(of_ref[...], wfo[...], preferred_element_type=F32)
    z = gate_ref[:, :d].astype(F32) * ya + gate_ref[:, d:].astype(F32) * yf
    x = x_ref[...] + jnp.dot(z.astype(BF16), wo[...], preferred_element_type=F32)
    x_o[...] = x
    ms = jnp.mean(x * x, axis=-1, keepdims=True)
    h2 = x * lax.rsqrt(ms + EPS) * g2_ref[...]
    h2_o[...] = h2

    lg = jnp.dot(h2, wr_ref[...], preferred_element_type=F32, precision=lax.Precision.HIGHEST) + br_ref[...]
    lane = lax.broadcasted_iota(jnp.int32, lg.shape, 1).astype(F32)
    is_g = lane < N_GROUPS
    gl = jnp.where(is_g, lg, NEG_BIG)
    gmax = jnp.max(gl, axis=-1, keepdims=True)
    gsum = jnp.sum(jnp.where(is_g, jnp.exp(gl - gmax), 0.0), axis=-1, keepdims=True)
    p_g = 1.0 / gsum
    g_idx = jnp.min(jnp.where(gl == gmax, lane, float(LANES)), axis=-1, keepdims=True)
    lo = N_GROUPS + EXPERTS_PER_GROUP * g_idx
    in_grp = jnp.abs(lane - lo - 0.5 * (EXPERTS_PER_GROUP - 1)) < 0.5 * EXPERTS_PER_GROUP
    el = jnp.where(in_grp, lg, NEG_BIG)
    emax = jnp.max(el, axis=-1, keepdims=True)
    esum = jnp.sum(jnp.where(in_grp, jnp.exp(el - emax), 0.0), axis=-1, keepdims=True)
    i0 = jnp.min(jnp.where(el == emax, lane, float(LANES)), axis=-1, keepdims=True)
    el2 = jnp.where(lane == i0, NEG_BIG, el)
    e2max = jnp.max(el2, axis=-1, keepdims=True)
    i1 = jnp.min(jnp.where(el2 == e2max, lane, float(LANES)), axis=-1, keepdims=True)
    p0 = 1.0 / esum
    p1 = jnp.exp(e2max - emax) / esum
    w0 = p_g * (p0 / (p0 + p1))
    w1 = p_g * (p1 / (p0 + p1))
    e0 = i0 - N_GROUPS
    e1 = i1 - N_GROUPS
    route_o[...] = jnp.where(lane == 0, e0, jnp.where(lane == 1, e1, jnp.where(lane == 2, w0,
                             jnp.where(lane == 3, w1, 0.0))))


def _post(x2d, oa, of, gates, wda, wfo, wo, g2, wr, br, tm):
    t, d = x2d.shape
    row_spec = lambda w: pl.BlockSpec((tm, w), lambda i: (i, 0))
    return pl.pallas_call(
        functools.partial(_post_kernel, tm=tm),
        grid=(t // tm,),
        in_specs=[row_spec(d), row_spec(oa.shape[1]), row_spec(of.shape[1]), row_spec(gates.shape[1]),
                  _const_spec(wda.shape), _const_spec(wfo.shape), _const_spec(wo.shape), _const_spec(g2.shape),
                  _const_spec(wr.shape), _const_spec(br.shape)],
        out_specs=[row_spec(d), row_spec(d), row_spec(LANES)],
        out_shape=[jax.ShapeDtypeStruct((t, d), F32), jax.ShapeDtypeStruct((t, d), F32),
                   jax.ShapeDtypeStruct((t, LANES), F32)],
        compiler_params=_cparams(("arbitrary",)), name="post_attn",
    )(x2d, oa, of, gates, wda, wfo, wo, g2, wr, br)


def _row_copy(src_hbm, idx, dst, r, sem):
    return pltpu.make_async_copy(src_hbm.at[pl.ds(idx, 1), :], dst.at[pl.ds(r, 1), :], sem)


def _gather_rows(idx_ref, src_hbm, dst, sem, n):
    def issue(r, c):
        _row_copy(src_hbm, idx_ref[0, 0, r], dst, r, sem).start()
        return c
    lax.fori_loop(0, n, issue, 0, unroll=8)

    def wait(r, c):
        _row_copy(src_hbm, 0, dst, r, sem).wait()
        return c
    lax.fori_loop(0, n, wait, 0, unroll=8)


def _moe_kernel(te_ref, tv_ref, src_ref, h2_hbm, wg_ref, wu_ref, wd_ref, y_ref, xbuf, sem, *, tm):
    i = pl.program_id(0)

    @pl.when(tv_ref[i] == 1)
    def _():
        _gather_rows(src_ref, h2_hbm, xbuf, sem, tm)
        x = xbuf[...].astype(BF16)
        a = jnp.dot(x, wg_ref[0].astype(BF16), preferred_element_type=F32)
        u = jnp.dot(x, wu_ref[0].astype(BF16), preferred_element_type=F32)
        hid = (a * jax.nn.sigmoid(a) * u).astype(BF16)
        y_ref[...] = jnp.dot(hid, wd_ref[0].astype(BF16), preferred_element_type=F32)

    @pl.when(tv_ref[i] == 0)
    def _():
        y_ref[...] = jnp.zeros_like(y_ref)


def _moe(tile_expert, tile_valid, src3d, h2, wg, wu, wd, tm):
    n_tiles = src3d.shape[0]
    d, f = wg.shape[1], wg.shape[2]
    grid_spec = pltpu.PrefetchScalarGridSpec(
        num_scalar_prefetch=2, grid=(n_tiles,),
        in_specs=[pl.BlockSpec((1, 1, tm), lambda i, te, tv: (i, 0, 0), memory_space=pltpu.SMEM),
                  pl.BlockSpec(memory_space=pl.ANY),
                  pl.BlockSpec((1, d, f), lambda i, te, tv: (te[i], 0, 0)),
                  pl.BlockSpec((1, d, f), lambda i, te, tv: (te[i], 0, 0)),
                  pl.BlockSpec((1, f, d), lambda i, te, tv: (te[i], 0, 0))],
        out_specs=pl.BlockSpec((tm, d), lambda i, te, tv: (i, 0)),
        scratch_shapes=[pltpu.VMEM((tm, d), F32), pltpu.SemaphoreType.DMA(())])
    return pl.pallas_call(
        functools.partial(_moe_kernel, tm=tm), grid_spec=grid_spec,
        out_shape=jax.ShapeDtypeStruct((n_tiles * tm, d), F32),
        compiler_params=_cparams(("arbitrary",)), name="moe_experts",
    )(tile_expert, tile_valid, src3d, h2, wg, wu, wd)


def _combine_kernel(dst_ref, x_ref, route_ref, y_hbm, o_ref, ybuf, sem, *, tm):
    _gather_rows(dst_ref, y_hbm, ybuf, sem, 2 * tm)
    w0 = route_ref[:, 2:3]
    w1 = route_ref[:, 3:4]
    o_ref[...] = x_ref[...] + w0 * ybuf[:tm, :] + w1 * ybuf[tm:, :]


def _combine(dst3d, x2d, route, y, tm):
    t, d = x2d.shape
    return pl.pallas_call(
        functools.partial(_combine_kernel, tm=tm),
        grid=(t // tm,),
        in_specs=[pl.BlockSpec((1, 1, 2 * tm), lambda i: (i, 0, 0), memory_space=pltpu.SMEM),
                  pl.BlockSpec((tm, d), lambda i: (i, 0)),
                  pl.BlockSpec((tm, LANES), lambda i: (i, 0)),
                  pl.BlockSpec(memory_space=pl.ANY)],
        out_specs=pl.BlockSpec((tm, d), lambda i: (i, 0)),
        out_shape=jax.ShapeDtypeStruct((t, d), F32),
        scratch_shapes=[pltpu.VMEM((2 * tm, d), F32), pltpu.SemaphoreType.DMA(())],
        compiler_params=_cparams(("arbitrary",)), name="moe_combine",
    )(dst3d, x2d, route, y)


def _dispatch_plan(route, tm_e, tm_c):
    t = route.shape[0]
    n_assign = 2 * t
    e = route[:, :2].astype(jnp.int32).reshape(-1)
    order = jnp.argsort(e, stable=True).astype(jnp.int32)
    counts = jnp.sum((e[:, None] == jnp.arange(N_EXPERTS, dtype=jnp.int32)[None, :]).astype(jnp.int32), axis=0)
    padded = ((counts + tm_e - 1) // tm_e) * tm_e
    pad_end = jnp.cumsum(padded)
    pad_off = pad_end - padded
    start = jnp.cumsum(counts) - counts
    n_tiles = n_assign // tm_e + N_EXPERTS
    tile_row0 = jnp.arange(n_tiles, dtype=jnp.int32) * tm_e
    tile_valid = (tile_row0 < pad_end[-1]).astype(jnp.int32)
    tile_expert = jnp.minimum(jnp.searchsorted(pad_end, tile_row0, side="right"), N_EXPERTS - 1).astype(jnp.int32)
    rows = jnp.arange(n_tiles * tm_e, dtype=jnp.int32)
    row_e = tile_expert[rows // tm_e]
    j = rows - pad_off[row_e]
    row_ok = (j < counts[row_e]) & (tile_valid[rows // tm_e] == 1)
    src_assign = order[jnp.clip(start[row_e] + j, 0, n_assign - 1)]
    src_tok = jnp.where(row_ok, src_assign // 2, 0).astype(jnp.int32)
    rank = jnp.zeros((n_assign,), jnp.int32).at[order].set(jnp.arange(n_assign, dtype=jnp.int32))
    dest = (pad_off[e] + rank - start[e]).astype(jnp.int32)
    dst3d = dest.reshape(t // tm_c, tm_c, 2).transpose(0, 2, 1).reshape(t // tm_c, 1, 2 * tm_c)
    return tile_expert, tile_valid, src_tok.reshape(n_tiles, 1, tm_e), dst3d


def _rotary_tables(positions):
    half = ROT_DIM // 2
    inv_freq = ROPE_THETA ** (-jnp.arange(0, ROT_DIM, 2, dtype=F32) / ROT_DIM)
    ang = positions.astype(F32).reshape(-1)[:, None] * inv_freq
    cos, sin = jnp.cos(ang), jnp.sin(ang)
    t = ang.shape[0]
    ones = jnp.ones((t, HEAD_DIM - ROT_DIM), F32)
    zeros_h = jnp.zeros((t, half), F32)
    zeros_r = jnp.zeros((t, HEAD_DIM - ROT_DIM), F32)
    cos_t = jnp.concatenate([cos, cos, ones], axis=1)
    sina_t = jnp.concatenate([-sin, zeros_h, zeros_r], axis=1)
    sinb_t = jnp.concatenate([zeros_h, sin, zeros_r], axis=1)
    rep = LANES // HEAD_DIM
    return tuple(jnp.tile(a, (1, rep)) for a in (cos_t, sina_t, sinb_t))


def _split_heads(a, b, s, nh):
    return a.reshape(b, s, nh, a.shape[-1] // nh).transpose(0, 2, 1, 3)


def kernel(x, positions, ln1_g, w_in, b_forget, q_norm_diff_g, k_norm_diff_g, lam_q1, lam_k1, lam_q2, lam_k2,
           subln_g, q_norm_fox_g, k_norm_fox_g, w_diff_out, w_fox_out, w_o, ln2_g, w_router_group,
           b_router_group, w_router_expert, b_router_expert, w_exp_gate, w_exp_up, w_exp_down):
    b, s, d = x.shape
    depth = w_in.shape[0]
    t = b * s
    tm = min(512, s)
    tq = min(512, s)
    tm_e = 256
    tm_c = min(256, s)
    assert s % tm == 0 and s % tq == 0 and t % tm_c == 0 and (2 * t) % tm_e == 0

    qk_w = 2 * DIFF_HEADS * HEAD_DIM
    v_w = DIFF_HEADS * 2 * HEAD_DIM
    f_w = FOX_HEADS * HEAD_DIM
    bounds = [0]
    for w in (qk_w, qk_w, v_w, f_w, f_w, f_w, FOX_HEADS, 2 * d):
        bounds.append(bounds[-1] + w)

    cos, sina, sinb = _rotary_tables(positions)
    hid = jnp.arange(qk_w, dtype=jnp.int32) // HEAD_DIM
    bd = jnp.where(hid[:, None] == hid[None, :], 1.0 / HEAD_DIM, 0.0).astype(BF16)
    tile_g = lambda g, n: jnp.tile(g.astype(F32), n)[None, :]

    x2d = x.reshape(t, d)
    for l in range(depth):
        lambda_init = 0.8 - 0.6 * math.exp(-0.3 * l)
        wl = w_in[l]
        ws = [wl[:, bounds[k]:bounds[k + 1]].astype(BF16) for k in range(8)]
        ws[6] = jnp.pad(ws[6], ((0, 0), (0, LANES - FOX_HEADS)))
        bfz = jnp.pad(b_forget[l].astype(F32), (0, LANES - FOX_HEADS))[None, :]
        qa, ka, va, qf, kf, vf, cum, gates = _inproj(
            x2d, ln1_g[l][None, :], ws, bd,
            tile_g(q_norm_diff_g[l], 2 * DIFF_HEADS), tile_g(k_norm_diff_g[l], 2 * DIFF_HEADS),
            tile_g(q_norm_fox_g[l], FOX_HEADS), tile_g(k_norm_fox_g[l], FOX_HEADS),
            bfz, cos, sina, sinb, s, tm)

        oa = _diff_attention(_split_heads(qa, b, s, 2 * DIFF_HEADS), _split_heads(ka, b, s, 2 * DIFF_HEADS),
                             va.reshape(b, s, v_w),
                             lam_q1[l][None, :].astype(F32), lam_k1[l][None, :].astype(F32),
                             lam_q2[l][None, :].astype(F32), lam_k2[l][None, :].astype(F32),
                             subln_g[l][None, :].astype(F32), lambda_init, tq)
        cum_col = cum.reshape(b, s, LANES)
        cum_row = cum_col[:, :, :FOX_HEADS].transpose(0, 2, 1)
        of = _fox_attention(_split_heads(qf, b, s, FOX_HEADS), _split_heads(kf, b, s, FOX_HEADS),
                            _split_heads(vf, b, s, FOX_HEADS), cum_col, cum_row, tq)

        wr = jnp.pad(jnp.concatenate([w_router_group[l], w_router_expert[l]], axis=1).astype(F32),
                     ((0, 0), (0, LANES - N_GROUPS - N_EXPERTS)))
        br = jnp.pad(jnp.concatenate([b_router_group[l], b_router_expert[l]]).astype(F32),
                     (0, LANES - N_GROUPS - N_EXPERTS))[None, :]
        x_mid, h2, route = _post(x2d, oa.reshape(t, v_w), of.reshape(t, f_w), gates,
                                 w_diff_out[l].astype(BF16), w_fox_out[l].astype(BF16), w_o[l].astype(BF16),
                                 ln2_g[l][None, :], wr, br, tm)

        tile_expert, tile_valid, src3d, dst3d = _dispatch_plan(route, tm_e, tm_c)
        f = w_exp_gate.shape[-1]
        y = _moe(tile_expert, tile_valid, src3d, h2,
                 w_exp_gate[l].reshape(N_EXPERTS, d, f), w_exp_up[l].reshape(N_EXPERTS, d, f),
                 w_exp_down[l].reshape(N_EXPERTS, f, d), tm_e)
        x2d = _combine(dst3d, x_mid, route, y, tm_c)
    return x2d.reshape(b, s, d)
```

```python
import functools
import math

import jax
import jax.numpy as jnp
from jax import lax
from jax.experimental import pallas as pl
from jax.experimental.pallas import tpu as pltpu

F32 = jnp.float32
BF16 = jnp.bfloat16

HEAD_DIM = 64
DIFF_HEADS = 4
FOX_HEADS = 8
ROT_DIM = HEAD_DIM // 4
ROPE_THETA = 500000.0
N_GROUPS = 4
EXPERTS_PER_GROUP = 8
N_EXPERTS = N_GROUPS * EXPERTS_PER_GROUP
EPS = 1e-6
LANES = 128
NEG_BIG = -1e30

VMEM_LIMIT = 56 * 1024 * 1024


def _cparams(sem):
    return pltpu.CompilerParams(dimension_semantics=sem, vmem_limit_bytes=VMEM_LIMIT)


def _const_spec(shape):
    nd = len(shape)
    return pl.BlockSpec(shape, lambda *_: (0,) * nd)


def _inproj_kernel(x_ref, g1_ref, wqa, wka, wva, wqf, wkf, wvf, wfz, wgz, bd_ref, gqa, gka, gqf, gkf, bf_ref,
                   cos_ref, sina_ref, sinb_ref,
                   qa_o, ka_o, va_o, qf_o, kf_o, vf_o, cum_o, gate_o, carry_ref, *, tiles_per_seq, tm):
    i = pl.program_id(0)
    x = x_ref[...]
    ms = jnp.mean(x * x, axis=-1, keepdims=True)
    h = (x * lax.rsqrt(ms + EPS) * g1_ref[...]).astype(BF16)

    def proj(w_ref):
        return jnp.dot(h, w_ref[...], preferred_element_type=F32)

    def head_norm(y, g_ref):
        hms = jnp.dot((y * y).astype(BF16), bd_ref[...], preferred_element_type=F32)
        return y * lax.rsqrt(hms + EPS) * g_ref[...]

    reps = qa_o.shape[1] // LANES
    cos = jnp.concatenate([cos_ref[...]] * reps, axis=1)
    sina = jnp.concatenate([sina_ref[...]] * reps, axis=1)
    sinb = jnp.concatenate([sinb_ref[...]] * reps, axis=1)
    width = qa_o.shape[1]
    half = ROT_DIM // 2

    def rotary(y):
        return y * cos + pltpu.roll(y, width - half, 1) * sina + pltpu.roll(y, half, 1) * sinb

    scale = HEAD_DIM ** -0.5
    qa_o[...] = (rotary(head_norm(proj(wqa), gqa)) * scale).astype(BF16)
    ka_o[...] = rotary(head_norm(proj(wka), gka)).astype(BF16)
    va_o[...] = proj(wva).astype(BF16)
    qf_o[...] = (head_norm(proj(wqf), gqf) * scale).astype(BF16)
    kf_o[...] = head_norm(proj(wkf), gkf).astype(BF16)
    vf_o[...] = proj(wvf).astype(BF16)

    z = proj(wfz) + bf_ref[...]
    ls = jnp.minimum(z, 0.0) - jnp.log(1.0 + jnp.exp(-jnp.abs(z)))
    row = lax.broadcasted_iota(jnp.int32, (tm, tm), 0)
    col = lax.broadcasted_iota(jnp.int32, (tm, tm), 1)
    tri = jnp.where(row >= col, 1.0, 0.0).astype(BF16)
    p1 = ls.astype(BF16)
    r1 = ls - p1.astype(F32)
    p2 = r1.astype(BF16)
    p3 = (r1 - p2.astype(F32)).astype(BF16)
    csum = (jnp.dot(tri, p1, preferred_element_type=F32) + jnp.dot(tri, p2, preferred_element_type=F32)
            + jnp.dot(tri, p3, preferred_element_type=F32))

    @pl.when(i % tiles_per_seq == 0)
    def _():
        carry_ref[...] = jnp.zeros_like(carry_ref)

    cum = csum + carry_ref[...]
    cum_o[...] = cum
    carry_ref[...] = cum[tm - 1:tm, :]

    gw = gate_o.shape[1]
    step = 512
    for c in range(gw // step):
        zz = jnp.dot(h, wgz[:, c * step:(c + 1) * step], preferred_element_type=F32)
        gate_o[:, c * step:(c + 1) * step] = jax.nn.sigmoid(zz).astype(BF16)


def _inproj(x2d, g1, ws, bd, gqa, gka, gqf, gkf, bfz, cos, sina, sinb, seq, tm):
    t, d = x2d.shape
    wqa, wka, wva, wqf, wkf, wvf, wfz, wgz = ws
    grid = (t // tm,)
    row_spec = lambda w: pl.BlockSpec((tm, w), lambda i: (i, 0))
    in_specs = [row_spec(d), _const_spec(g1.shape)] + [_const_spec(w.shape) for w in ws] + [
        _const_spec(bd.shape), _const_spec(gqa.shape), _const_spec(gka.shape), _const_spec(gqf.shape),
        _const_spec(gkf.shape), _const_spec(bfz.shape), row_spec(LANES), row_spec(LANES), row_spec(LANES)]
    widths = [wqa.shape[1], wka.shape[1], wva.shape[1], wqf.shape[1], wkf.shape[1], wvf.shape[1]]
    out_shape = [jax.ShapeDtypeStruct((t, w), BF16) for w in widths] + [
        jax.ShapeDtypeStruct((t, LANES), F32), jax.ShapeDtypeStruct((t, wgz.shape[1]), BF16)]
    out_specs = [row_spec(w) for w in widths] + [row_spec(LANES), row_spec(wgz.shape[1])]
    return pl.pallas_call(
        functools.partial(_inproj_kernel, tiles_per_seq=seq // tm, tm=tm),
        grid=grid, in_specs=in_specs, out_specs=out_specs, out_shape=out_shape,
        scratch_shapes=[pltpu.VMEM((1, LANES), F32)],
        compiler_params=_cparams(("arbitrary",)), name="inproj",
    )(x2d, g1, *ws, bd, gqa, gka, gqf, gkf, bfz, cos, sina, sinb)


def _flash_update(q, k, v, carry, bias, masked):
    m, l, acc = carry
    s = lax.dot_general(q, k, (((1,), (1,)), ((), ())), preferred_element_type=F32)
    if bias is not None:
        s = s + bias
    if masked:
        row = lax.broadcasted_iota(jnp.int32, s.shape, 0)
        col = lax.broadcasted_iota(jnp.int32, s.shape, 1)
        s = jnp.where(col <= row, s, NEG_BIG)
    m_new = jnp.maximum(m, jnp.max(s, axis=-1, keepdims=True))
    alpha = jnp.exp(m - m_new)
    p = jnp.exp(s - m_new)
    l = alpha * l + jnp.sum(p, axis=-1, keepdims=True)
    acc = alpha * acc + jnp.dot(p.astype(BF16), v, preferred_element_type=F32)
    return m_new, l, acc


def _flash_init(tq, dv):
    return (jnp.full((tq, 1), NEG_BIG, F32), jnp.zeros((tq, 1), F32), jnp.zeros((tq, dv), F32))


def _diff_attn_kernel(q1_ref, q2_ref, k1_ref, k2_ref, v_ref, lq1, lk1, lq2, lk2, sg_ref, o_ref, *, tq, lambda_init):
    qi = pl.program_id(2)
    q1 = q1_ref[0, 0]
    q2 = q2_ref[0, 0]
    dv = v_ref.shape[2]

    def step(j, carry, masked):
        c1, c2 = carry
        ks = pl.ds(pl.multiple_of(j * tq, tq), tq)
        v = v_ref[0, ks, :]
        c1 = _flash_update(q1, k1_ref[0, 0, ks, :], v, c1, None, masked)
        c2 = _flash_update(q2, k2_ref[0, 0, ks, :], v, c2, None, masked)
        return c1, c2

    carry = (_flash_init(tq, dv), _flash_init(tq, dv))
    carry = lax.fori_loop(0, qi, lambda j, c: step(j, c, False), carry)
    (_, l1, a1), (_, l2, a2) = step(qi, carry, True)

    lam = (jnp.exp(jnp.sum(lq1[...] * lk1[...], axis=-1, keepdims=True))
           - jnp.exp(jnp.sum(lq2[...] * lk2[...], axis=-1, keepdims=True)) + lambda_init)
    o = a1 / l1 - lam * (a2 / l2)
    ms = jnp.mean(o * o, axis=-1, keepdims=True)
    o = o * lax.rsqrt(ms + EPS) * sg_ref[...] * (1.0 - lambda_init)
    o_ref[0] = o.astype(o_ref.dtype)


def _diff_attention(qh, kh, va, lq1, lk1, lq2, lk2, subg, lambda_init, tq):
    b, _, s, dk = qh.shape
    dv = 2 * HEAD_DIM
    grid = (b, DIFF_HEADS, s // tq)
    q_spec = lambda off: pl.BlockSpec((1, 1, tq, dk), lambda bi, hi, qi: (bi, hi + off, qi, 0))
    k_spec = lambda off: pl.BlockSpec((1, 1, s, dk), lambda bi, hi, qi: (bi, hi + off, 0, 0))
    vec = lambda a: pl.BlockSpec(a.shape, lambda bi, hi, qi: (0, 0))
    return pl.pallas_call(
        functools.partial(_diff_attn_kernel, tq=tq, lambda_init=lambda_init),
        grid=grid,
        in_specs=[q_spec(0), q_spec(DIFF_HEADS), k_spec(0), k_spec(DIFF_HEADS),
                  pl.BlockSpec((1, s, dv), lambda bi, hi, qi: (bi, 0, hi)),
                  vec(lq1), vec(lk1), vec(lq2), vec(lk2), vec(subg)],
        out_specs=pl.BlockSpec((1, tq, dv), lambda bi, hi, qi: (bi, qi, hi)),
        out_shape=jax.ShapeDtypeStruct((b, s, DIFF_HEADS * dv), BF16),
        compiler_params=_cparams(("arbitrary", "arbitrary", "arbitrary")), name="diff_attn",
    )(qh, qh, kh, kh, va, lq1, lk1, lq2, lk2, subg)


def _fox_attn_kernel(q_ref, k_ref, v_ref, ccol_ref, crow_ref, o_ref, *, tq):
    hp = pl.program_id(1)
    qi = pl.program_id(2)
    dv = v_ref.shape[3]
    outs = []
    for u in range(2):
        q = q_ref[0, u]
        lane = lax.broadcasted_iota(jnp.int32, ccol_ref.shape[1:], 1)
        ct = jnp.sum(jnp.where(lane == 2 * hp + u, ccol_ref[0], 0.0), axis=-1, keepdims=True)

        def step(j, carry, masked, u=u, q=q, ct=ct):
            ks = pl.ds(pl.multiple_of(j * tq, tq), tq)
            cs = crow_ref[0, pl.ds(2 * hp + u, 1), ks]
            return _flash_update(q, k_ref[0, u, ks, :], v_ref[0, u, ks, :], carry, ct - cs, masked)

        carry = lax.fori_loop(0, qi, lambda j, c: step(j, c, False), _flash_init(tq, dv))
        _, l, a = step(qi, carry, True)
        outs.append(a / l)
    o_ref[0] = jnp.concatenate(outs, axis=-1).astype(o_ref.dtype)


def _fox_attention(qh, kh, vh, cum_col, cum_row, tq):
    b, nh, s, dk = qh.shape
    grid = (b, nh // 2, s // tq)
    return pl.pallas_call(
        functools.partial(_fox_attn_kernel, tq=tq),
        grid=grid,
        in_specs=[pl.BlockSpec((1, 2, tq, dk), lambda bi, hp, qi: (bi, hp, qi, 0)),
                  pl.BlockSpec((1, 2, s, dk), lambda bi, hp, qi: (bi, hp, 0, 0)),
                  pl.BlockSpec((1, 2, s, dk), lambda bi, hp, qi: (bi, hp, 0, 0)),
                  pl.BlockSpec((1, tq, LANES), lambda bi, hp, qi: (bi, qi, 0)),
                  pl.BlockSpec((1, nh, s), lambda bi, hp, qi: (bi, 0, 0))],
        out_specs=pl.BlockSpec((1, tq, 2 * dk), lambda bi, hp, qi: (bi, qi, hp)),
        out_shape=jax.ShapeDtypeStruct((b, s, nh * dk), BF16),
        compiler_params=_cparams(("arbitrary", "arbitrary", "arbitrary")), name="fox_attn",
    )(qh, kh, vh, cum_col, cum_row)


ROUTE_ROWS = 8
EXPERT_ROW0 = 8


def _post_kernel(x_ref, oa_ref, of_ref, gate_ref, wda, wfo, wo, g2_ref, wrt_ref, brt_ref,
                 x_o, h2_o, route_o, cnt_o, cnt_ref, *, tm):
    i = pl.program_id(0)
    d = x_ref.shape[1]
    ya = jnp.dot(oa_ref[...], wda[...], preferred_element_type=F32)
    yf = jnp.dot(of_ref[...], wfo[...], preferred_element_type=F32)
    z = gate_ref[:, :d].astype(F32) * ya + gate_ref[:, d:].astype(F32) * yf
    x = x_ref[...] + jnp.dot(z.astype(BF16), wo[...], preferred_element_type=F32)
    x_o[...] = x
    ms = jnp.mean(x * x, axis=-1, keepdims=True)
    h2 = x * lax.rsqrt(ms + EPS) * g2_ref[...]
    h2_o[...] = h2

    lg = lax.dot_general(wrt_ref[...], h2, (((1,), (1,)), ((), ())), preferred_element_type=F32,
                         precision=lax.Precision.HIGHEST) + brt_ref[...]
    sub = lax.broadcasted_iota(jnp.int32, (EXPERTS_PER_GROUP, tm), 0).astype(F32)
    gl = jnp.where(sub < N_GROUPS, lg[0:EXPERTS_PER_GROUP], NEG_BIG)
    gmax = jnp.max(gl, axis=0, keepdims=True)
    p_g = 1.0 / jnp.sum(jnp.exp(gl - gmax), axis=0, keepdims=True)
    g_idx = jnp.min(jnp.where(gl == gmax, sub, float(EXPERTS_PER_GROUP)), axis=0, keepdims=True)
    el = lg[EXPERT_ROW0:EXPERT_ROW0 + EXPERTS_PER_GROUP]
    for g in range(1, N_GROUPS):
        lo = EXPERT_ROW0 + EXPERTS_PER_GROUP * g
        el = jnp.where(g_idx == g, lg[lo:lo + EXPERTS_PER_GROUP], el)
    emax = jnp.max(el, axis=0, keepdims=True)
    esum = jnp.sum(jnp.exp(el - emax), axis=0, keepdims=True)
    i0 = jnp.min(jnp.where(el == emax, sub, float(EXPERTS_PER_GROUP)), axis=0, keepdims=True)
    el2 = jnp.where(sub == i0, NEG_BIG, el)
    e2max = jnp.max(el2, axis=0, keepdims=True)
    i1 = jnp.min(jnp.where(el2 == e2max, sub, float(EXPERTS_PER_GROUP)), axis=0, keepdims=True)
    p0 = 1.0 / esum
    p1 = jnp.exp(e2max - emax) / esum
    w0 = p_g * (p0 / (p0 + p1))
    w1 = p_g * (p1 / (p0 + p1))
    e0 = g_idx * EXPERTS_PER_GROUP + i0
    e1 = g_idx * EXPERTS_PER_GROUP + i1

    @pl.when(i == 0)
    def _():
        cnt_ref[...] = jnp.zeros_like(cnt_ref)

    erow = lax.broadcasted_iota(jnp.int32, (N_EXPERTS, tm), 0).astype(F32)
    oh0 = jnp.where(erow == e0, 1.0, 0.0)
    oh1 = jnp.where(erow == e1, 1.0, 0.0)
    both = oh0 + oh1
    r = lax.broadcasted_iota(jnp.int32, (tm, tm), 0)
    c = lax.broadcasted_iota(jnp.int32, (tm, tm), 1)
    earlier = jnp.where(r < c, 1.0, 0.0).astype(BF16)
    base = jnp.dot(both.astype(BF16), earlier, preferred_element_type=F32) + cnt_ref[:, 0:1]
    rank0 = jnp.sum(oh0 * base, axis=0, keepdims=True)
    rank1 = jnp.sum(oh1 * base, axis=0, keepdims=True)
    cnt_ref[...] = cnt_ref[...] + jnp.sum(both, axis=1, keepdims=True)
    cnt_o[...] = cnt_ref[...]

    route_o[...] = jnp.where(sub == 0, e0, jnp.where(sub == 1, e1, jnp.where(sub == 2, w0, jnp.where(
        sub == 3, w1, jnp.where(sub == 4, rank0, jnp.where(sub == 5, rank1, 0.0))))))


def _post(x2d, oa, of, gates, wda, wfo, wo, g2, wrt, brt, tm):
    t, d = x2d.shape
    row_spec = lambda w: pl.BlockSpec((tm, w), lambda i: (i, 0))
    return pl.pallas_call(
        functools.partial(_post_kernel, tm=tm),
        grid=(t // tm,),
        in_specs=[row_spec(d), row_spec(oa.shape[1]), row_spec(of.shape[1]), row_spec(gates.shape[1]),
                  _const_spec(wda.shape), _const_spec(wfo.shape), _const_spec(wo.shape), _const_spec(g2.shape),
                  _const_spec(wrt.shape), _const_spec(brt.shape)],
        out_specs=[row_spec(d), row_spec(d), pl.BlockSpec((ROUTE_ROWS, tm), lambda i: (0, i)),
                   _const_spec((N_EXPERTS, LANES))],
        out_shape=[jax.ShapeDtypeStruct((t, d), F32), jax.ShapeDtypeStruct((t, d), F32),
                   jax.ShapeDtypeStruct((ROUTE_ROWS, t), F32), jax.ShapeDtypeStruct((N_EXPERTS, LANES), F32)],
        scratch_shapes=[pltpu.VMEM((N_EXPERTS, LANES), F32)],
        compiler_params=_cparams(("arbitrary",)), name="post_attn",
    )(x2d, oa, of, gates, wda, wfo, wo, g2, wrt, brt)


def _dispatch_kernel(pos_ref, h2_ref, xs_in, xs_hbm, sem, *, tm):
    del xs_in

    def copy(r, k):
        return pltpu.make_async_copy(h2_ref.at[pl.ds(r, 1), :], xs_hbm.at[pl.ds(pos_ref[0, 0, k * tm + r], 1), :], sem)

    def issue(r, c):
        copy(r, 0).start()
        copy(r, 1).start()
        return c
    lax.fori_loop(0, tm, issue, 0, unroll=8)

    def wait(r, c):
        copy(r, 0).wait()
        copy(r, 1).wait()
        return c
    lax.fori_loop(0, tm, wait, 0, unroll=8)


def _dispatch(pos3d, h2, xs_zero, tm):
    t, d = h2.shape
    return pl.pallas_call(
        functools.partial(_dispatch_kernel, tm=tm),
        grid=(t // tm,),
        in_specs=[pl.BlockSpec((1, 1, 2 * tm), lambda i: (i, 0, 0), memory_space=pltpu.SMEM),
                  pl.BlockSpec((tm, d), lambda i: (i, 0)),
                  pl.BlockSpec(memory_space=pl.ANY)],
        out_specs=pl.BlockSpec(memory_space=pl.ANY),
        out_shape=jax.ShapeDtypeStruct(xs_zero.shape, xs_zero.dtype),
        scratch_shapes=[pltpu.SemaphoreType.DMA(())],
        input_output_aliases={2: 0},
        compiler_params=_cparams(("arbitrary",)), name="moe_dispatch",
    )(pos3d, h2, xs_zero)


def _moe_kernel(te_ref, tv_ref, x_ref, wg_ref, wu_ref, wd_ref, y_ref):
    i = pl.program_id(0)

    @pl.when(tv_ref[i] == 1)
    def _():
        x = x_ref[...].astype(BF16)
        a = jnp.dot(x, wg_ref[0].astype(BF16), preferred_element_type=F32)
        u = jnp.dot(x, wu_ref[0].astype(BF16), preferred_element_type=F32)
        hid = (a * jax.nn.sigmoid(a) * u).astype(BF16)
        y_ref[...] = jnp.dot(hid, wd_ref[0].astype(BF16), preferred_element_type=F32)

    @pl.when(tv_ref[i] == 0)
    def _():
        y_ref[...] = jnp.zeros_like(y_ref)


def _moe(tile_expert, tile_valid, xs, wg, wu, wd, tm):
    n_tiles = xs.shape[0] // tm
    d, f = wg.shape[1], wg.shape[2]
    grid_spec = pltpu.PrefetchScalarGridSpec(
        num_scalar_prefetch=2, grid=(n_tiles,),
        in_specs=[pl.BlockSpec((tm, d), lambda i, te, tv: (i, 0)),
                  pl.BlockSpec((1, d, f), lambda i, te, tv: (te[i], 0, 0)),
                  pl.BlockSpec((1, d, f), lambda i, te, tv: (te[i], 0, 0)),
                  pl.BlockSpec((1, f, d), lambda i, te, tv: (te[i], 0, 0))],
        out_specs=pl.BlockSpec((tm, d), lambda i, te, tv: (i, 0)))
    return pl.pallas_call(
        _moe_kernel, grid_spec=grid_spec,
        out_shape=jax.ShapeDtypeStruct((n_tiles * tm, d), F32),
        compiler_params=_cparams(("arbitrary",)), name="moe_experts",
    )(tile_expert, tile_valid, xs, wg, wu, wd)


def _combine_kernel(pos_ref, x_ref, route_ref, y_hbm, o_ref, ybuf, sem, *, tm):
    def copy(r):
        return pltpu.make_async_copy(y_hbm.at[pl.ds(pos_ref[0, 0, r], 1), :], ybuf.at[pl.ds(r, 1), :], sem)

    def issue(r, c):
        copy(r).start()
        return c
    lax.fori_loop(0, 2 * tm, issue, 0, unroll=8)

    def wait(r, c):
        copy(r).wait()
        return c
    lax.fori_loop(0, 2 * tm, wait, 0, unroll=8)

    d = x_ref.shape[1]
    w0 = jnp.tile(jnp.broadcast_to(route_ref[2:3, :], (LANES, tm)).T, (1, d // LANES))
    w1 = jnp.tile(jnp.broadcast_to(route_ref[3:4, :], (LANES, tm)).T, (1, d // LANES))
    o_ref[...] = x_ref[...] + w0 * ybuf[:tm, :] + w1 * ybuf[tm:, :]


def _combine(pos3d, x2d, route, y, tm):
    t, d = x2d.shape
    return pl.pallas_call(
        functools.partial(_combine_kernel, tm=tm),
        grid=(t // tm,),
        in_specs=[pl.BlockSpec((1, 1, 2 * tm), lambda i: (i, 0, 0), memory_space=pltpu.SMEM),
                  pl.BlockSpec((tm, d), lambda i: (i, 0)),
                  pl.BlockSpec((ROUTE_ROWS, tm), lambda i: (0, i)),
                  pl.BlockSpec(memory_space=pl.ANY)],
        out_specs=pl.BlockSpec((tm, d), lambda i: (i, 0)),
        out_shape=jax.ShapeDtypeStruct((t, d), F32),
        scratch_shapes=[pltpu.VMEM((2 * tm, d), F32), pltpu.SemaphoreType.DMA(())],
        compiler_params=_cparams(("arbitrary",)), name="moe_combine",
    )(pos3d, x2d, route, y)


def _dispatch_plan(route, counts, tm_e, tm_c):
    t = route.shape[1]
    counts = counts.astype(jnp.int32)
    padded = ((counts + tm_e - 1) // tm_e) * tm_e
    pad_end = jnp.cumsum(padded)
    pad_off = pad_end - padded
    n_tiles = (2 * t) // tm_e + N_EXPERTS
    tile_row0 = jnp.arange(n_tiles, dtype=jnp.int32) * tm_e
    tile_valid = (tile_row0 < pad_end[-1]).astype(jnp.int32)
    tile_expert = jnp.minimum(jnp.sum((tile_row0[:, None] >= pad_end[None, :]).astype(jnp.int32), axis=1),
                              N_EXPERTS - 1).astype(jnp.int32)
    e_ids = route[0:2].astype(jnp.int32)
    ranks = route[4:6].astype(jnp.int32)
    onehot = e_ids[:, :, None] == jnp.arange(N_EXPERTS, dtype=jnp.int32)[None, None, :]
    pos = ranks + jnp.sum(jnp.where(onehot, pad_off[None, None, :], 0), axis=-1)
    pos3d = pos.reshape(2, t // tm_c, tm_c).transpose(1, 0, 2).reshape(t // tm_c, 1, 2 * tm_c)
    return tile_expert, tile_valid, pos3d, n_tiles


def _rotary_tables(positions):
    half = ROT_DIM // 2
    inv_freq = ROPE_THETA ** (-jnp.arange(0, ROT_DIM, 2, dtype=F32) / ROT_DIM)
    ang = positions.astype(F32).reshape(-1)[:, None] * inv_freq
    cos, sin = jnp.cos(ang), jnp.sin(ang)
    t = ang.shape[0]
    ones = jnp.ones((t, HEAD_DIM - ROT_DIM), F32)
    zeros_h = jnp.zeros((t, half), F32)
    zeros_r = jnp.zeros((t, HEAD_DIM - ROT_DIM), F32)
    cos_t = jnp.concatenate([cos, cos, ones], axis=1)
    sina_t = jnp.concatenate([-sin, zeros_h, zeros_r], axis=1)
    sinb_t = jnp.concatenate([zeros_h, sin, zeros_r], axis=1)
    rep = LANES // HEAD_DIM
    return tuple(jnp.tile(a, (1, rep)) for a in (cos_t, sina_t, sinb_t))


def _split_heads(a, b, s, nh):
    return a.reshape(b, s, nh, a.shape[-1] // nh).transpose(0, 2, 1, 3)


def kernel(x, positions, ln1_g, w_in, b_forget, q_norm_diff_g, k_norm_diff_g, lam_q1, lam_k1, lam_q2, lam_k2,
           subln_g, q_norm_fox_g, k_norm_fox_g, w_diff_out, w_fox_out, w_o, ln2_g, w_router_group,
           b_router_group, w_router_expert, b_router_expert, w_exp_gate, w_exp_up, w_exp_down):
    b, s, d = x.shape
    depth = w_in.shape[0]
    t = b * s
    tm = min(512, s)
    tq = min(512, s)
    tm_e = 256
    tm_c = min(256, s)
    assert s % tm == 0 and s % tq == 0 and t % tm_c == 0 and (2 * t) % tm_e == 0

    qk_w = 2 * DIFF_HEADS * HEAD_DIM
    v_w = DIFF_HEADS * 2 * HEAD_DIM
    f_w = FOX_HEADS * HEAD_DIM
    bounds = [0]
    for w in (qk_w, qk_w, v_w, f_w, f_w, f_w, FOX_HEADS, 2 * d):
        bounds.append(bounds[-1] + w)

    cos, sina, sinb = _rotary_tables(positions)
    hid = jnp.arange(qk_w, dtype=jnp.int32) // HEAD_DIM
    bd = jnp.where(hid[:, None] == hid[None, :], 1.0 / HEAD_DIM, 0.0).astype(BF16)
    tile_g = lambda g, n: jnp.tile(g.astype(F32), n)[None, :]

    x2d = x.reshape(t, d)
    for l in range(depth):
        lambda_init = 0.8 - 0.6 * math.exp(-0.3 * l)
        wl = w_in[l]
        ws = [wl[:, bounds[k]:bounds[k + 1]].astype(BF16) for k in range(8)]
        ws[6] = jnp.pad(ws[6], ((0, 0), (0, LANES - FOX_HEADS)))
        bfz = jnp.pad(b_forget[l].astype(F32), (0, LANES - FOX_HEADS))[None, :]
        qa, ka, va, qf, kf, vf, cum, gates = _inproj(
            x2d, ln1_g[l][None, :], ws, bd,
            tile_g(q_norm_diff_g[l], 2 * DIFF_HEADS), tile_g(k_norm_diff_g[l], 2 * DIFF_HEADS),
            tile_g(q_norm_fox_g[l], FOX_HEADS), tile_g(k_norm_fox_g[l], FOX_HEADS),
            bfz, cos, sina, sinb, s, tm)

        oa = _diff_attention(_split_heads(qa, b, s, 2 * DIFF_HEADS), _split_heads(ka, b, s, 2 * DIFF_HEADS),
                             va.reshape(b, s, v_w),
                             lam_q1[l][None, :].astype(F32), lam_k1[l][None, :].astype(F32),
                             lam_q2[l][None, :].astype(F32), lam_k2[l][None, :].astype(F32),
                             subln_g[l][None, :].astype(F32), lambda_init, tq)
        cum_col = cum.reshape(b, s, LANES)
        cum_row = cum_col[:, :, :FOX_HEADS].transpose(0, 2, 1)
        of = _fox_attention(_split_heads(qf, b, s, FOX_HEADS), _split_heads(kf, b, s, FOX_HEADS),
                            _split_heads(vf, b, s, FOX_HEADS), cum_col, cum_row, tq)

        wrt = jnp.zeros((LANES, d), F32)
        wrt = wrt.at[:N_GROUPS].set(w_router_group[l].T.astype(F32))
        wrt = wrt.at[EXPERT_ROW0:EXPERT_ROW0 + N_EXPERTS].set(w_router_expert[l].T.astype(F32))
        brt = jnp.zeros((LANES, 1), F32)
        brt = brt.at[:N_GROUPS, 0].set(b_router_group[l].astype(F32))
        brt = brt.at[EXPERT_ROW0:EXPERT_ROW0 + N_EXPERTS, 0].set(b_router_expert[l].astype(F32))
        x_mid, h2, route, counts = _post(x2d, oa.reshape(t, v_w), of.reshape(t, f_w), gates,
                                         w_diff_out[l].astype(BF16), w_fox_out[l].astype(BF16),
                                         w_o[l].astype(BF16), ln2_g[l][None, :], wrt, brt, tm)

        tile_expert, tile_valid, pos3d, n_tiles = _dispatch_plan(route, counts[:, 0], tm_e, tm_c)
        xs = _dispatch(pos3d, h2, jnp.zeros((n_tiles * tm_e, d), F32), tm_c)
        f = w_exp_gate.shape[-1]
        y = _moe(tile_expert, tile_valid, xs,
                 w_exp_gate[l].reshape(N_EXPERTS, d, f), w_exp_up[l].reshape(N_EXPERTS, d, f),
                 w_exp_down[l].reshape(N_EXPERTS, f, d), tm_e)
        x2d = _combine(pos3d, x_mid, route, y, tm_c)
    return x2d.reshape(b, s, d)
```

```python
import functools
import math

import jax
import jax.numpy as jnp
from jax import lax
from jax.experimental import pallas as pl
from jax.experimental.pallas import tpu as pltpu

F32 = jnp.float32
BF16 = jnp.bfloat16

HEAD_DIM = 64
DIFF_HEADS = 4
FOX_HEADS = 8
ROT_DIM = HEAD_DIM // 4
ROPE_THETA = 500000.0
N_GROUPS = 4
EXPERTS_PER_GROUP = 8
N_EXPERTS = N_GROUPS * EXPERTS_PER_GROUP
EPS = 1e-6
LANES = 128
SUBLANES = 8
NEG_BIG = -1e30
LOG2E = math.log2(math.e)
Q_SCALE = HEAD_DIM ** -0.5 * LOG2E
BIAS_PARTS = 3

VMEM_LIMIT = 56 * 1024 * 1024


def _cparams(sem):
    return pltpu.CompilerParams(dimension_semantics=sem, vmem_limit_bytes=VMEM_LIMIT)


def _const_spec(shape):
    nd = len(shape)
    return pl.BlockSpec(shape, lambda *_: (0,) * nd)


def _nt_dot(a, b):
    return lax.dot_general(a, b, (((1,), (1,)), ((), ())), preferred_element_type=F32)


def _split3(a):
    p1 = a.astype(BF16)
    r1 = a - p1.astype(F32)
    p2 = r1.astype(BF16)
    p3 = (r1 - p2.astype(F32)).astype(BF16)
    return p1, p2, p3


def _inproj_kernel(x_ref, g1_ref, wqa, wka, wva, wqf, wkf, wvf, wfz, wgz, gqa, gka, gqf, gkf, bf_ref,
                   cos_ref, sin_ref,
                   qa_o, ka_o, va_o, qf_o, kf_o, vf_o, caux_o, gate_o, carry_ref, *, tiles_per_seq, tm):
    i = pl.program_id(0)
    x = x_ref[...]
    ms = jnp.mean(x * x, axis=-1, keepdims=True)
    h = (x * lax.rsqrt(ms + EPS) * g1_ref[...]).astype(BF16)
    cos = cos_ref[...]
    sin = sin_ref[...]
    half = ROT_DIM // 2

    def proj_t(w_ref):
        return _nt_dot(w_ref[...], h)

    def head_norm(y, g_ref, rotate, scale):
        g = jnp.broadcast_to(g_ref[...], (HEAD_DIM, tm))
        outs = []
        for j in range(y.shape[0] // HEAD_DIM):
            yj = y[j * HEAD_DIM:(j + 1) * HEAD_DIM]
            hms = jnp.mean(yj * yj, axis=0, keepdims=True)
            yn = yj * lax.rsqrt(hms + EPS) * g
            if rotate:
                t1, t2 = yn[0:half], yn[half:ROT_DIM]
                yn = jnp.concatenate([t1 * cos - t2 * sin, t2 * cos + t1 * sin, yn[ROT_DIM:]], axis=0)
            outs.append(yn * scale if scale != 1.0 else yn)
        return jnp.concatenate(outs, axis=0)

    qa_o[...] = head_norm(proj_t(wqa), gqa, True, Q_SCALE).astype(BF16)
    ka_o[...] = head_norm(proj_t(wka), gka, True, 1.0).T.astype(BF16)
    va_o[...] = proj_t(wva).astype(BF16)
    qf_o[...] = head_norm(proj_t(wqf), gqf, False, Q_SCALE).astype(BF16)
    kf_o[...] = head_norm(proj_t(wkf), gkf, False, 1.0).T.astype(BF16)
    vf_o[...] = proj_t(wvf).astype(BF16)

    z = proj_t(wfz)[0:FOX_HEADS] + bf_ref[...]
    ls = jnp.minimum(z, 0.0) - jnp.log(1.0 + jnp.exp(-jnp.abs(z)))
    r = lax.broadcasted_iota(jnp.int32, (tm, tm), 0)
    c = lax.broadcasted_iota(jnp.int32, (tm, tm), 1)
    upto = jnp.where(r <= c, 1.0, 0.0).astype(BF16)
    csum = sum(jnp.dot(p, upto, preferred_element_type=F32) for p in _split3(ls))

    @pl.when(i % tiles_per_seq == 0)
    def _():
        carry_ref[...] = jnp.zeros_like(carry_ref)

    cum = csum + carry_ref[:, 0:1]
    carry_ref[...] = jnp.broadcast_to(cum[:, tm - 1:tm], carry_ref.shape)
    parts = [p.astype(F32) for p in _split3(-LOG2E * cum)]
    pad = jnp.zeros((LANES - BIAS_PARTS * FOX_HEADS, tm), F32)
    caux_o[...] = jnp.concatenate(parts + [pad], axis=0).T.astype(BF16)

    gw = gate_o.shape[1]
    step = 512
    for cb in range(gw // step):
        zz = jnp.dot(h, wgz[:, cb * step:(cb + 1) * step], preferred_element_type=F32)
        gate_o[:, cb * step:(cb + 1) * step] = jax.nn.sigmoid(zz).astype(BF16)


def _inproj(x2d, g1, wts, wgz, gains, bfz, cos_t, sin_t, seq, tm):
    t, d = x2d.shape
    row_spec = lambda w: pl.BlockSpec((tm, w), lambda i: (i, 0))
    col_spec = lambda r: pl.BlockSpec((r, tm), lambda i: (0, i))
    in_specs = ([row_spec(d), _const_spec(g1.shape)] + [_const_spec(w.shape) for w in wts] + [_const_spec(wgz.shape)]
                + [_const_spec(g.shape) for g in gains] + [_const_spec(bfz.shape), col_spec(cos_t.shape[0]),
                                                           col_spec(sin_t.shape[0])])
    fw = wts[0].shape[0]
    tr = jax.ShapeDtypeStruct((fw, t), BF16)
    rm = jax.ShapeDtypeStruct((t, fw), BF16)
    out_shape = [tr, rm, tr, tr, rm, tr, jax.ShapeDtypeStruct((t, LANES), BF16),
                 jax.ShapeDtypeStruct((t, wgz.shape[1]), BF16)]
    out_specs = [col_spec(fw), row_spec(fw), col_spec(fw), col_spec(fw), row_spec(fw), col_spec(fw),
                 row_spec(LANES), row_spec(wgz.shape[1])]
    return pl.pallas_call(
        functools.partial(_inproj_kernel, tiles_per_seq=seq // tm, tm=tm),
        grid=(t // tm,), in_specs=in_specs, out_specs=out_specs, out_shape=out_shape,
        scratch_shapes=[pltpu.VMEM((FOX_HEADS, LANES), F32)],
        compiler_params=_cparams(("arbitrary",)), name="inproj",
    )(x2d, g1, *wts, wgz, *gains, bfz, cos_t, sin_t)


DIFF_KEY_SPLIT = 2
FOX_KEY_SPLIT = 4
ONES_ROWS = 2 * SUBLANES


def _masked_scores(kx, w, masked):
    s = jnp.dot(kx, w, preferred_element_type=F32)
    if masked:
        key = lax.broadcasted_iota(jnp.int32, s.shape, 0)
        qry = lax.broadcasted_iota(jnp.int32, s.shape, 1)
        s = jnp.where(key <= qry, s, NEG_BIG)
    return s


def _absorb(s, v_ext, m, acc_ref, c, q_lo):
    nq = s.shape[1]
    m_new = jnp.maximum(m, jnp.max(s, axis=0, keepdims=True))
    alpha = jnp.exp2(m - m_new)
    p = jnp.exp2((s - m_new).astype(BF16))
    acc_ref[c, :, q_lo:q_lo + nq] = (alpha * acc_ref[c, :, q_lo:q_lo + nq]
                                     + jnp.dot(v_ext, p, preferred_element_type=F32))
    return m_new


def _causal_sweep(qi, n_chains, tq, key_split, keys, values, ws, acc_ref):
    half = tq // key_split
    chains = range(n_chains)

    def block(k0, ms, masked):
        subs = tuple((k0 + i * half, i * half if masked else 0) for i in range(key_split))
        scores = [[_masked_scores(keys(ks, half, c), ws[c][:, q_lo:], masked) for c in chains]
                  for ks, q_lo in subs]
        for (ks, q_lo), ss in zip(subs, scores):
            new = [_absorb(ss[c], values(ks, half, c), ms[c][:, q_lo:], acc_ref, c, q_lo) for c in chains]
            ms = tuple(jnp.concatenate([ms[c][:, :q_lo], new[c]], axis=1) if q_lo else new[c] for c in chains)
        return ms

    init = tuple(jnp.full((1, tq), NEG_BIG, F32) for _ in chains)
    ms = lax.fori_loop(0, qi, lambda j, ms: block(pl.multiple_of(j * tq, tq), ms, False), init)
    block(pl.multiple_of(qi * tq, tq), ms, True)


def _diff_attn_kernel(q1_ref, q2_ref, k1_ref, k2_ref, v_ref, lq1, lk1, lq2, lk2, sg_ref, o_ref, acc_ref,
                      *, tq, lambda_init):
    qi = pl.program_id(2)
    dv = 2 * HEAD_DIM
    zeros = jnp.zeros((HEAD_DIM, tq), BF16)
    ws = []
    for q_ref in (q1_ref, q2_ref):
        ws.append(jnp.concatenate([q_ref[0:HEAD_DIM, :], zeros], axis=0))
        ws.append(jnp.concatenate([zeros, q_ref[HEAD_DIM:2 * HEAD_DIM, :]], axis=0))
    acc_ref[...] = jnp.zeros_like(acc_ref)

    def keys(k0, tks, c):
        return (k1_ref, k1_ref, k2_ref, k2_ref)[c][pl.ds(pl.multiple_of(k0, tks), tks), :]

    def values(k0, tks, c):
        u = c % 2
        v = v_ref[u * dv:(u + 1) * dv, pl.ds(pl.multiple_of(k0, tks), tks)]
        return jnp.concatenate([v, jnp.ones((ONES_ROWS, tks), BF16)], axis=0)

    _causal_sweep(qi, 4, tq, DIFF_KEY_SPLIT, keys, values, ws, acc_ref)
    lam = (jnp.exp(jnp.sum(lq1[...] * lk1[...], axis=0, keepdims=True))
           - jnp.exp(jnp.sum(lq2[...] * lk2[...], axis=0, keepdims=True)) + lambda_init)
    sg = sg_ref[...] * (1.0 - lambda_init)
    outs = []
    for c1, c2 in ((0, 2), (1, 3)):
        o = (acc_ref[c1, 0:dv, :] / acc_ref[c1, dv:dv + 1, :]
             - lam * (acc_ref[c2, 0:dv, :] / acc_ref[c2, dv:dv + 1, :]))
        oms = jnp.mean(o * o, axis=0, keepdims=True)
        outs.append((o * lax.rsqrt(oms + EPS) * sg).T)
    o_ref[...] = jnp.concatenate(outs, axis=1).astype(o_ref.dtype)


def _diff_attention(qa_t, ka, va_t, lq1, lk1, lq2, lk2, subg, lambda_init, b, s, tq):
    t = b * s
    nq = s // tq
    pair = 2 * HEAD_DIM
    dv = 2 * HEAD_DIM
    n_pairs = DIFF_HEADS // 2
    q_spec = lambda off: pl.BlockSpec((pair, tq), lambda bi, hp, qi: (hp + off, bi * nq + qi))
    k_spec = lambda off: pl.BlockSpec((s, pair), lambda bi, hp, qi: (bi, hp + off))
    vec = lambda a: pl.BlockSpec(a.shape, lambda bi, hp, qi: (0, 0))
    return pl.pallas_call(
        functools.partial(_diff_attn_kernel, tq=tq, lambda_init=lambda_init),
        grid=(b, n_pairs, nq),
        in_specs=[q_spec(0), q_spec(n_pairs), k_spec(0), k_spec(n_pairs),
                  pl.BlockSpec((2 * dv, s), lambda bi, hp, qi: (hp, bi)),
                  vec(lq1), vec(lk1), vec(lq2), vec(lk2), vec(subg)],
        out_specs=pl.BlockSpec((tq, 2 * dv), lambda bi, hp, qi: (bi * nq + qi, hp)),
        out_shape=jax.ShapeDtypeStruct((t, DIFF_HEADS * dv), BF16),
        scratch_shapes=[pltpu.VMEM((4, dv + ONES_ROWS, tq), F32)],
        compiler_params=_cparams(("arbitrary", "arbitrary", "arbitrary")), name="diff_attn",
    )(qa_t, qa_t, ka, ka, va_t, lq1, lk1, lq2, lk2, subg)


def _fox_attn_kernel(q_ref, k_ref, caux_ref, v_ref, o_ref, acc_ref, *, tq):
    hp = pl.program_id(1)
    qi = pl.program_id(2)
    zeros = jnp.zeros((HEAD_DIM, tq), BF16)
    row = lax.broadcasted_iota(jnp.int32, (LANES, tq), 0)
    ws = []
    for u in range(2):
        head = 2 * hp + u
        pick = jnp.where(row < BIAS_PARTS * FOX_HEADS, jnp.where(row % FOX_HEADS == head, 1.0, 0.0), 0.0).astype(BF16)
        qh = q_ref[u * HEAD_DIM:(u + 1) * HEAD_DIM, :]
        ws.append(jnp.concatenate([qh, zeros, pick] if u == 0 else [zeros, qh, pick], axis=0))
    acc_ref[...] = jnp.zeros_like(acc_ref)

    def keys(k0, tks, c):
        ks = pl.ds(pl.multiple_of(k0, tks), tks)
        return jnp.concatenate([k_ref[ks, :], caux_ref[ks, :]], axis=1)

    def values(k0, tks, c):
        v = v_ref[c * HEAD_DIM:(c + 1) * HEAD_DIM, pl.ds(pl.multiple_of(k0, tks), tks)]
        return jnp.concatenate([v, jnp.ones((ONES_ROWS, tks), BF16)], axis=0)

    _causal_sweep(qi, 2, tq, FOX_KEY_SPLIT, keys, values, ws, acc_ref)
    o = jnp.concatenate([acc_ref[u, 0:HEAD_DIM, :] / acc_ref[u, HEAD_DIM:HEAD_DIM + 1, :] for u in range(2)],
                        axis=0)
    o_ref[...] = o.T.astype(o_ref.dtype)


def _fox_attention(qf_t, kf, caux, vf_t, b, s, tq):
    t = b * s
    nq = s // tq
    pair = 2 * HEAD_DIM
    return pl.pallas_call(
        functools.partial(_fox_attn_kernel, tq=tq),
        grid=(b, FOX_HEADS // 2, nq),
        in_specs=[pl.BlockSpec((pair, tq), lambda bi, hp, qi: (hp, bi * nq + qi)),
                  pl.BlockSpec((s, pair), lambda bi, hp, qi: (bi, hp)),
                  pl.BlockSpec((s, LANES), lambda bi, hp, qi: (bi, 0)),
                  pl.BlockSpec((pair, s), lambda bi, hp, qi: (hp, bi))],
        out_specs=pl.BlockSpec((tq, pair), lambda bi, hp, qi: (bi * nq + qi, hp)),
        out_shape=jax.ShapeDtypeStruct((t, FOX_HEADS * HEAD_DIM), BF16),
        scratch_shapes=[pltpu.VMEM((2, HEAD_DIM + ONES_ROWS, tq), F32)],
        compiler_params=_cparams(("arbitrary", "arbitrary", "arbitrary")), name="fox_attn",
    )(qf_t, kf, caux, vf_t)


ROUTE_ROWS = 8
EXPERT_ROW0 = 8


def _post_kernel(x_ref, oa_ref, of_ref, gate_ref, wda, wfo, wo, g2_ref, wrt_ref, brt_ref,
                 x_o, h2_o, route_o, cnt_o, cnt_ref, *, tm):
    i = pl.program_id(0)
    d = x_ref.shape[1]
    ya = jnp.dot(oa_ref[...], wda[...], preferred_element_type=F32)
    yf = jnp.dot(of_ref[...], wfo[...], preferred_element_type=F32)
    z = gate_ref[:, :d].astype(F32) * ya + gate_ref[:, d:].astype(F32) * yf
    x = x_ref[...] + jnp.dot(z.astype(BF16), wo[...], preferred_element_type=F32)
    x_o[...] = x
    ms = jnp.mean(x * x, axis=-1, keepdims=True)
    h2 = x * lax.rsqrt(ms + EPS) * g2_ref[...]
    h2_o[...] = h2

    lg = lax.dot_general(wrt_ref[...], h2, (((1,), (1,)), ((), ())), preferred_element_type=F32,
                         precision=lax.Precision.HIGHEST) + brt_ref[...]
    sub = lax.broadcasted_iota(jnp.int32, (EXPERTS_PER_GROUP, tm), 0).astype(F32)
    gl = jnp.where(sub < N_GROUPS, lg[0:EXPERTS_PER_GROUP], NEG_BIG)
    gmax = jnp.max(gl, axis=0, keepdims=True)
    p_g = 1.0 / jnp.sum(jnp.exp(gl - gmax), axis=0, keepdims=True)
    g_idx = jnp.min(jnp.where(gl == gmax, sub, float(EXPERTS_PER_GROUP)), axis=0, keepdims=True)
    el = lg[EXPERT_ROW0:EXPERT_ROW0 + EXPERTS_PER_GROUP]
    for g in range(1, N_GROUPS):
        lo = EXPERT_ROW0 + EXPERTS_PER_GROUP * g
        el = jnp.where(g_idx == g, lg[lo:lo + EXPERTS_PER_GROUP], el)
    emax = jnp.max(el, axis=0, keepdims=True)
    esum = jnp.sum(jnp.exp(el - emax), axis=0, keepdims=True)
    i0 = jnp.min(jnp.where(el == emax, sub, float(EXPERTS_PER_GROUP)), axis=0, keepdims=True)
    el2 = jnp.where(sub == i0, NEG_BIG, el)
    e2max = jnp.max(el2, axis=0, keepdims=True)
    i1 = jnp.min(jnp.where(el2 == e2max, sub, float(EXPERTS_PER_GROUP)), axis=0, keepdims=True)
    p0 = 1.0 / esum
    p1 = jnp.exp(e2max - emax) / esum
    w0 = p_g * (p0 / (p0 + p1))
    w1 = p_g * (p1 / (p0 + p1))
    e0 = g_idx * EXPERTS_PER_GROUP + i0
    e1 = g_idx * EXPERTS_PER_GROUP + i1

    @pl.when(i == 0)
    def _():
        cnt_ref[...] = jnp.zeros_like(cnt_ref)

    erow = lax.broadcasted_iota(jnp.int32, (N_EXPERTS, tm), 0).astype(F32)
    oh0 = jnp.where(erow == e0, 1.0, 0.0)
    oh1 = jnp.where(erow == e1, 1.0, 0.0)
    both = oh0 + oh1
    r = lax.broadcasted_iota(jnp.int32, (tm, tm), 0)
    c = lax.broadcasted_iota(jnp.int32, (tm, tm), 1)
    earlier = jnp.where(r < c, 1.0, 0.0).astype(BF16)
    base = jnp.dot(both.astype(BF16), earlier, preferred_element_type=F32) + cnt_ref[:, 0:1]
    rank0 = jnp.sum(oh0 * base, axis=0, keepdims=True)
    rank1 = jnp.sum(oh1 * base, axis=0, keepdims=True)
    cnt_ref[...] = cnt_ref[...] + jnp.sum(both, axis=1, keepdims=True)
    cnt_o[...] = cnt_ref[...]

    route_o[...] = jnp.where(sub == 0, e0, jnp.where(sub == 1, e1, jnp.where(sub == 2, w0, jnp.where(
        sub == 3, w1, jnp.where(sub == 4, rank0, jnp.where(sub == 5, rank1, 0.0))))))


def _post(x2d, oa, of, gates, wda, wfo, wo, g2, wrt, brt, tm):
    t, d = x2d.shape
    row_spec = lambda w: pl.BlockSpec((tm, w), lambda i: (i, 0))
    return pl.pallas_call(
        functools.partial(_post_kernel, tm=tm),
        grid=(t // tm,),
        in_specs=[row_spec(d), row_spec(oa.shape[1]), row_spec(of.shape[1]), row_spec(gates.shape[1]),
                  _const_spec(wda.shape), _const_spec(wfo.shape), _const_spec(wo.shape), _const_spec(g2.shape),
                  _const_spec(wrt.shape), _const_spec(brt.shape)],
        out_specs=[row_spec(d), row_spec(d), pl.BlockSpec((ROUTE_ROWS, tm), lambda i: (0, i)),
                   _const_spec((N_EXPERTS, LANES))],
        out_shape=[jax.ShapeDtypeStruct((t, d), F32), jax.ShapeDtypeStruct((t, d), F32),
                   jax.ShapeDtypeStruct((ROUTE_ROWS, t), F32), jax.ShapeDtypeStruct((N_EXPERTS, LANES), F32)],
        scratch_shapes=[pltpu.VMEM((N_EXPERTS, LANES), F32)],
        compiler_params=_cparams(("arbitrary",)), name="post_attn",
    )(x2d, oa, of, gates, wda, wfo, wo, g2, wrt, brt)


def _dispatch_kernel(pos_ref, h2_ref, xs_in, xs_hbm, sem, *, tm):
    del xs_in

    def copy(r, k):
        return pltpu.make_async_copy(h2_ref.at[pl.ds(r, 1), :], xs_hbm.at[pl.ds(pos_ref[0, 0, k * tm + r], 1), :], sem)

    def issue(r, c):
        copy(r, 0).start()
        copy(r, 1).start()
        return c
    lax.fori_loop(0, tm, issue, 0, unroll=8)

    def wait(r, c):
        copy(r, 0).wait()
        copy(r, 1).wait()
        return c
    lax.fori_loop(0, tm, wait, 0, unroll=8)


def _dispatch(pos3d, h2, xs_zero, tm):
    t, d = h2.shape
    return pl.pallas_call(
        functools.partial(_dispatch_kernel, tm=tm),
        grid=(t // tm,),
        in_specs=[pl.BlockSpec((1, 1, 2 * tm), lambda i: (i, 0, 0), memory_space=pltpu.SMEM),
                  pl.BlockSpec((tm, d), lambda i: (i, 0)),
                  pl.BlockSpec(memory_space=pl.ANY)],
        out_specs=pl.BlockSpec(memory_space=pl.ANY),
        out_shape=jax.ShapeDtypeStruct(xs_zero.shape, xs_zero.dtype),
        scratch_shapes=[pltpu.SemaphoreType.DMA(())],
        input_output_aliases={2: 0},
        compiler_params=_cparams(("arbitrary",)), name="moe_dispatch",
    )(pos3d, h2, xs_zero)


def _moe_kernel(te_ref, tv_ref, x_ref, wg_ref, wu_ref, wd_ref, y_ref):
    i = pl.program_id(0)

    @pl.when(tv_ref[i] == 1)
    def _():
        x = x_ref[...].astype(BF16)
        a = jnp.dot(x, wg_ref[0].astype(BF16), preferred_element_type=F32)
        u = jnp.dot(x, wu_ref[0].astype(BF16), preferred_element_type=F32)
        hid = (a * jax.nn.sigmoid(a) * u).astype(BF16)
        y_ref[...] = jnp.dot(hid, wd_ref[0].astype(BF16), preferred_element_type=F32)

    @pl.when(tv_ref[i] == 0)
    def _():
        y_ref[...] = jnp.zeros_like(y_ref)


def _moe(tile_expert, tile_valid, xs, wg, wu, wd, tm):
    n_tiles = xs.shape[0] // tm
    d, f = wg.shape[1], wg.shape[2]
    grid_spec = pltpu.PrefetchScalarGridSpec(
        num_scalar_prefetch=2, grid=(n_tiles,),
        in_specs=[pl.BlockSpec((tm, d), lambda i, te, tv: (i, 0)),
                  pl.BlockSpec((1, d, f), lambda i, te, tv: (te[i], 0, 0)),
                  pl.BlockSpec((1, d, f), lambda i, te, tv: (te[i], 0, 0)),
                  pl.BlockSpec((1, f, d), lambda i, te, tv: (te[i], 0, 0))],
        out_specs=pl.BlockSpec((tm, d), lambda i, te, tv: (i, 0)))
    return pl.pallas_call(
        _moe_kernel, grid_spec=grid_spec,
        out_shape=jax.ShapeDtypeStruct((n_tiles * tm, d), F32),
        compiler_params=_cparams(("arbitrary",)), name="moe_experts",
    )(tile_expert, tile_valid, xs, wg, wu, wd)


def _combine_kernel(pos_ref, x_ref, route_ref, y_hbm, o_ref, ybuf, sem, *, tm):
    def copy(r):
        return pltpu.make_async_copy(y_hbm.at[pl.ds(pos_ref[0, 0, r], 1), :], ybuf.at[pl.ds(r, 1), :], sem)

    def issue(r, c):
        copy(r).start()
        return c
    lax.fori_loop(0, 2 * tm, issue, 0, unroll=8)

    def wait(r, c):
        copy(r).wait()
        return c
    lax.fori_loop(0, 2 * tm, wait, 0, unroll=8)

    d = x_ref.shape[1]
    w0 = jnp.tile(jnp.broadcast_to(route_ref[2:3, :], (LANES, tm)).T, (1, d // LANES))
    w1 = jnp.tile(jnp.broadcast_to(route_ref[3:4, :], (LANES, tm)).T, (1, d // LANES))
    o_ref[...] = x_ref[...] + w0 * ybuf[:tm, :] + w1 * ybuf[tm:, :]


def _combine(pos3d, x2d, route, y, tm):
    t, d = x2d.shape
    return pl.pallas_call(
        functools.partial(_combine_kernel, tm=tm),
        grid=(t // tm,),
        in_specs=[pl.BlockSpec((1, 1, 2 * tm), lambda i: (i, 0, 0), memory_space=pltpu.SMEM),
                  pl.BlockSpec((tm, d), lambda i: (i, 0)),
                  pl.BlockSpec((ROUTE_ROWS, tm), lambda i: (0, i)),
                  pl.BlockSpec(memory_space=pl.ANY)],
        out_specs=pl.BlockSpec((tm, d), lambda i: (i, 0)),
        out_shape=jax.ShapeDtypeStruct((t, d), F32),
        scratch_shapes=[pltpu.VMEM((2 * tm, d), F32), pltpu.SemaphoreType.DMA(())],
        compiler_params=_cparams(("arbitrary",)), name="moe_combine",
    )(pos3d, x2d, route, y)


def _dispatch_plan(route, counts, tm_e, tm_c):
    t = route.shape[1]
    counts = counts.astype(jnp.int32)
    padded = ((counts + tm_e - 1) // tm_e) * tm_e
    pad_end = jnp.cumsum(padded)
    pad_off = pad_end - padded
    n_tiles = (2 * t) // tm_e + N_EXPERTS
    tile_row0 = jnp.arange(n_tiles, dtype=jnp.int32) * tm_e
    tile_valid = (tile_row0 < pad_end[-1]).astype(jnp.int32)
    tile_expert = jnp.minimum(jnp.sum((tile_row0[:, None] >= pad_end[None, :]).astype(jnp.int32), axis=1),
                              N_EXPERTS - 1).astype(jnp.int32)
    e_ids = route[0:2].astype(jnp.int32)
    ranks = route[4:6].astype(jnp.int32)
    onehot = e_ids[:, :, None] == jnp.arange(N_EXPERTS, dtype=jnp.int32)[None, None, :]
    pos = ranks + jnp.sum(jnp.where(onehot, pad_off[None, None, :], 0), axis=-1)
    pos3d = pos.reshape(2, t // tm_c, tm_c).transpose(1, 0, 2).reshape(t // tm_c, 1, 2 * tm_c)
    return tile_expert, tile_valid, pos3d, n_tiles


def _rotary_tables(positions):
    inv_freq = ROPE_THETA ** (-jnp.arange(0, ROT_DIM, 2, dtype=F32) / ROT_DIM)
    ang = inv_freq[:, None] * positions.astype(F32).reshape(-1)[None, :]
    return jnp.cos(ang), jnp.sin(ang)


def kernel(x, positions, ln1_g, w_in, b_forget, q_norm_diff_g, k_norm_diff_g, lam_q1, lam_k1, lam_q2, lam_k2,
           subln_g, q_norm_fox_g, k_norm_fox_g, w_diff_out, w_fox_out, w_o, ln2_g, w_router_group,
           b_router_group, w_router_expert, b_router_expert, w_exp_gate, w_exp_up, w_exp_down):
    b, s, d = x.shape
    depth = w_in.shape[0]
    t = b * s
    tm = min(512, s)
    tq = min(512, s)
    tm_e = 256
    tm_c = min(256, s)
    assert s % tm == 0 and s % tq == 0 and t % tm_c == 0 and (2 * t) % tm_e == 0

    qk_w = 2 * DIFF_HEADS * HEAD_DIM
    v_w = DIFF_HEADS * 2 * HEAD_DIM
    f_w = FOX_HEADS * HEAD_DIM
    assert qk_w == v_w == f_w
    bounds = [0]
    for w in (qk_w, qk_w, v_w, f_w, f_w, f_w, FOX_HEADS, 2 * d):
        bounds.append(bounds[-1] + w)

    cos_t, sin_t = _rotary_tables(positions)
    col = lambda g: g.astype(F32)[:, None]

    x2d = x.reshape(t, d)
    for l in range(depth):
        lambda_init = 0.8 - 0.6 * math.exp(-0.3 * l)
        wl = w_in[l]
        wts = [wl[:, bounds[k]:bounds[k + 1]].T.astype(BF16) for k in range(7)]
        wts[6] = jnp.pad(wts[6], ((0, 2 * SUBLANES - FOX_HEADS), (0, 0)))
        wgz = wl[:, bounds[7]:bounds[8]].astype(BF16)
        gains = [col(q_norm_diff_g[l]), col(k_norm_diff_g[l]), col(q_norm_fox_g[l]), col(k_norm_fox_g[l])]
        qa_t, ka, va_t, qf_t, kf, vf_t, caux, gates = _inproj(
            x2d, ln1_g[l][None, :], wts, wgz, gains, col(b_forget[l]), cos_t, sin_t, s, tm)

        oa = _diff_attention(qa_t, ka, va_t, col(lam_q1[l]), col(lam_k1[l]), col(lam_q2[l]), col(lam_k2[l]),
                             col(subln_g[l]), lambda_init, b, s, tq)
        of = _fox_attention(qf_t, kf, caux, vf_t, b, s, tq)

        wrt = jnp.zeros((LANES, d), F32)
        wrt = wrt.at[:N_GROUPS].set(w_router_group[l].T.astype(F32))
        wrt = wrt.at[EXPERT_ROW0:EXPERT_ROW0 + N_EXPERTS].set(w_router_expert[l].T.astype(F32))
        brt = jnp.zeros((LANES, 1), F32)
        brt = brt.at[:N_GROUPS, 0].set(b_router_group[l].astype(F32))
        brt = brt.at[EXPERT_ROW0:EXPERT_ROW0 + N_EXPERTS, 0].set(b_router_expert[l].astype(F32))
        x_mid, h2, route, counts = _post(x2d, oa, of, gates,
                                         w_diff_out[l].astype(BF16), w_fox_out[l].astype(BF16),
                                         w_o[l].astype(BF16), ln2_g[l][None, :], wrt, brt, tm)

        tile_expert, tile_valid, pos3d, n_tiles = _dispatch_plan(route, counts[:, 0], tm_e, tm_c)
        xs = _dispatch(pos3d, h2, jnp.zeros((n_tiles * tm_e, d), F32), tm_c)
        f = w_exp_gate.shape[-1]
        y = _moe(tile_expert, tile_valid, xs,
                 w_exp_gate[l].reshape(N_EXPERTS, d, f), w_exp_up[l].reshape(N_EXPERTS, d, f),
                 w_exp_down[l].reshape(N_EXPERTS, f, d), tm_e)
        x2d = _combine(pos3d, x_mid, route, y, tm_c)
    return x2d.reshape(b, s, d)
```

```python
import functools
import math

import jax
import jax.numpy as jnp
from jax import lax
from jax.experimental import pallas as pl
from jax.experimental.pallas import tpu as pltpu

F32 = jnp.float32
BF16 = jnp.bfloat16

HEAD_DIM = 64
DIFF_HEADS = 4
FOX_HEADS = 8
ROT_DIM = HEAD_DIM // 4
ROPE_THETA = 500000.0
N_GROUPS = 4
EXPERTS_PER_GROUP = 8
N_EXPERTS = N_GROUPS * EXPERTS_PER_GROUP
EPS = 1e-6
LANES = 128
SUBLANES = 8
NEG_BIG = -1e30
LOG2E = math.log2(math.e)
Q_SCALE = HEAD_DIM ** -0.5 * LOG2E
BIAS_PARTS = 3

VMEM_LIMIT = 56 * 1024 * 1024


def _cparams(sem):
    return pltpu.CompilerParams(dimension_semantics=sem, vmem_limit_bytes=VMEM_LIMIT)


def _const_spec(shape):
    nd = len(shape)
    return pl.BlockSpec(shape, lambda *_: (0,) * nd)


def _nt_dot(a, b):
    return lax.dot_general(a, b, (((1,), (1,)), ((), ())), preferred_element_type=F32)


def _split3(a):
    p1 = a.astype(BF16)
    r1 = a - p1.astype(F32)
    p2 = r1.astype(BF16)
    p3 = (r1 - p2.astype(F32)).astype(BF16)
    return p1, p2, p3


def _inproj_kernel(x_ref, g1_ref, wqa, wka, wva, wqf, wkf, wvf, wfz, wgz, gqa, gka, gqf, gkf, bf_ref,
                   cos_ref, sin_ref,
                   qa_o, ka_o, va_o, qf_o, kf_o, vf_o, caux_o, gate_o, carry_ref, *, tiles_per_seq, tm):
    i = pl.program_id(0)
    x = x_ref[...]
    ms = jnp.mean(x * x, axis=-1, keepdims=True)
    h = (x * lax.rsqrt(ms + EPS) * g1_ref[...]).astype(BF16)
    cos = cos_ref[...]
    sin = sin_ref[...]
    half = ROT_DIM // 2

    def proj_t(w_ref):
        return _nt_dot(w_ref[...], h)

    def head_norm(y, g_ref, rotate, scale):
        g = jnp.broadcast_to(g_ref[...], (HEAD_DIM, tm))
        outs = []
        for j in range(y.shape[0] // HEAD_DIM):
            yj = y[j * HEAD_DIM:(j + 1) * HEAD_DIM]
            hms = jnp.mean(yj * yj, axis=0, keepdims=True)
            yn = yj * lax.rsqrt(hms + EPS) * g
            if rotate:
                t1, t2 = yn[0:half], yn[half:ROT_DIM]
                yn = jnp.concatenate([t1 * cos - t2 * sin, t2 * cos + t1 * sin, yn[ROT_DIM:]], axis=0)
            outs.append(yn * scale if scale != 1.0 else yn)
        return jnp.concatenate(outs, axis=0)

    qa_o[...] = head_norm(proj_t(wqa), gqa, True, Q_SCALE).astype(BF16)
    ka_o[...] = head_norm(proj_t(wka), gka, True, 1.0).T.astype(BF16)
    va_o[...] = proj_t(wva).astype(BF16)
    qf_o[...] = head_norm(proj_t(wqf), gqf, False, Q_SCALE).astype(BF16)
    kf_o[...] = head_norm(proj_t(wkf), gkf, False, 1.0).T.astype(BF16)
    vf_o[...] = proj_t(wvf).astype(BF16)

    z = proj_t(wfz)[0:FOX_HEADS] + bf_ref[...]
    ls = jnp.minimum(z, 0.0) - jnp.log(1.0 + jnp.exp(-jnp.abs(z)))
    r = lax.broadcasted_iota(jnp.int32, (tm, tm), 0)
    c = lax.broadcasted_iota(jnp.int32, (tm, tm), 1)
    upto = jnp.where(r <= c, 1.0, 0.0).astype(BF16)
    csum = sum(jnp.dot(p, upto, preferred_element_type=F32) for p in _split3(ls))

    @pl.when(i % tiles_per_seq == 0)
    def _():
        carry_ref[...] = jnp.zeros_like(carry_ref)

    cum = csum + carry_ref[:, 0:1]
    carry_ref[...] = jnp.broadcast_to(cum[:, tm - 1:tm], carry_ref.shape)
    parts = [p.astype(F32) for p in _split3(-LOG2E * cum)]
    pad = jnp.zeros((LANES - BIAS_PARTS * FOX_HEADS, tm), F32)
    caux_o[...] = jnp.concatenate(parts + [pad], axis=0).T.astype(BF16)

    gw = gate_o.shape[1]
    step = 512
    for cb in range(gw // step):
        zz = jnp.dot(h, wgz[:, cb * step:(cb + 1) * step], preferred_element_type=F32)
        gate_o[:, cb * step:(cb + 1) * step] = jax.nn.sigmoid(zz).astype(BF16)


def _inproj(x2d, g1, wts, wgz, gains, bfz, cos_t, sin_t, seq, tm):
    t, d = x2d.shape
    row_spec = lambda w: pl.BlockSpec((tm, w), lambda i: (i, 0))
    col_spec = lambda r: pl.BlockSpec((r, tm), lambda i: (0, i))
    in_specs = ([row_spec(d), _const_spec(g1.shape)] + [_const_spec(w.shape) for w in wts] + [_const_spec(wgz.shape)]
                + [_const_spec(g.shape) for g in gains] + [_const_spec(bfz.shape), col_spec(cos_t.shape[0]),
                                                           col_spec(sin_t.shape[0])])
    fw = wts[0].shape[0]
    tr = jax.ShapeDtypeStruct((fw, t), BF16)
    rm = jax.ShapeDtypeStruct((t, fw), BF16)
    out_shape = [tr, rm, tr, tr, rm, tr, jax.ShapeDtypeStruct((t, LANES), BF16),
                 jax.ShapeDtypeStruct((t, wgz.shape[1]), BF16)]
    out_specs = [col_spec(fw), row_spec(fw), col_spec(fw), col_spec(fw), row_spec(fw), col_spec(fw),
                 row_spec(LANES), row_spec(wgz.shape[1])]
    return pl.pallas_call(
        functools.partial(_inproj_kernel, tiles_per_seq=seq // tm, tm=tm),
        grid=(t // tm,), in_specs=in_specs, out_specs=out_specs, out_shape=out_shape,
        scratch_shapes=[pltpu.VMEM((FOX_HEADS, LANES), F32)],
        compiler_params=_cparams(("arbitrary",)), name="inproj",
    )(x2d, g1, *wts, wgz, *gains, bfz, cos_t, sin_t)


DIFF_KEY_SPLIT = 2
FOX_KEY_SPLIT = 4
ONES_ROWS = 2 * SUBLANES


def _masked_scores(kx, w, masked):
    s = jnp.dot(kx, w, preferred_element_type=F32)
    if masked:
        key = lax.broadcasted_iota(jnp.int32, s.shape, 0)
        qry = lax.broadcasted_iota(jnp.int32, s.shape, 1)
        s = jnp.where(key <= qry, s, NEG_BIG)
    return s


def _absorb(s, v_ext, m, acc_ref, c, q_lo):
    nq = s.shape[1]
    m_new = jnp.maximum(m, jnp.max(s, axis=0, keepdims=True))
    alpha = jnp.exp2(m - m_new)
    p = jnp.exp2((s - m_new).astype(BF16))
    acc_ref[c, :, q_lo:q_lo + nq] = (alpha * acc_ref[c, :, q_lo:q_lo + nq]
                                     + jnp.dot(v_ext, p, preferred_element_type=F32))
    return m_new


def _causal_sweep(qi, n_chains, tq, key_split, keys, values, ws, acc_ref):
    half = tq // key_split
    chains = range(n_chains)

    def block(k0, ms, masked):
        subs = tuple((k0 + i * half, i * half if masked else 0) for i in range(key_split))
        scores = [[_masked_scores(keys(ks, half, c), ws[c][:, q_lo:], masked) for c in chains]
                  for ks, q_lo in subs]
        for (ks, q_lo), ss in zip(subs, scores):
            new = [_absorb(ss[c], values(ks, half, c), ms[c][:, q_lo:], acc_ref, c, q_lo) for c in chains]
            ms = tuple(jnp.concatenate([ms[c][:, :q_lo], new[c]], axis=1) if q_lo else new[c] for c in chains)
        return ms

    init = tuple(jnp.full((1, tq), NEG_BIG, F32) for _ in chains)
    ms = lax.fori_loop(0, qi, lambda j, ms: block(pl.multiple_of(j * tq, tq), ms, False), init)
    block(pl.multiple_of(qi * tq, tq), ms, True)


def _diff_attn_kernel(q1_ref, q2_ref, k1_ref, k2_ref, v_ref, lq1, lk1, lq2, lk2, sg_ref, o_ref, acc_ref,
                      *, tq, lambda_init):
    qi = pl.program_id(2)
    dv = 2 * HEAD_DIM
    zeros = jnp.zeros((HEAD_DIM, tq), BF16)
    ws = []
    for q_ref in (q1_ref, q2_ref):
        ws.append(jnp.concatenate([q_ref[0:HEAD_DIM, :], zeros], axis=0))
        ws.append(jnp.concatenate([zeros, q_ref[HEAD_DIM:2 * HEAD_DIM, :]], axis=0))
    acc_ref[...] = jnp.zeros_like(acc_ref)

    def keys(k0, tks, c):
        return (k1_ref, k1_ref, k2_ref, k2_ref)[c][pl.ds(pl.multiple_of(k0, tks), tks), :]

    def values(k0, tks, c):
        u = c % 2
        v = v_ref[u * dv:(u + 1) * dv, pl.ds(pl.multiple_of(k0, tks), tks)]
        return jnp.concatenate([v, jnp.ones((ONES_ROWS, tks), BF16)], axis=0)

    _causal_sweep(qi, 4, tq, DIFF_KEY_SPLIT, keys, values, ws, acc_ref)
    lam = (jnp.exp(jnp.sum(lq1[...] * lk1[...], axis=0, keepdims=True))
           - jnp.exp(jnp.sum(lq2[...] * lk2[...], axis=0, keepdims=True)) + lambda_init)
    sg = sg_ref[...] * (1.0 - lambda_init)
    outs = []
    for c1, c2 in ((0, 2), (1, 3)):
        o = (acc_ref[c1, 0:dv, :] / acc_ref[c1, dv:dv + 1, :]
             - lam * (acc_ref[c2, 0:dv, :] / acc_ref[c2, dv:dv + 1, :]))
        oms = jnp.mean(o * o, axis=0, keepdims=True)
        outs.append((o * lax.rsqrt(oms + EPS) * sg).T)
    o_ref[...] = jnp.concatenate(outs, axis=1).astype(o_ref.dtype)


def _diff_attention(qa_t, ka, va_t, lq1, lk1, lq2, lk2, subg, lambda_init, b, s, tq):
    t = b * s
    nq = s // tq
    pair = 2 * HEAD_DIM
    dv = 2 * HEAD_DIM
    n_pairs = DIFF_HEADS // 2
    q_spec = lambda off: pl.BlockSpec((pair, tq), lambda bi, hp, qi: (hp + off, bi * nq + qi))
    k_spec = lambda off: pl.BlockSpec((s, pair), lambda bi, hp, qi: (bi, hp + off))
    vec = lambda a: pl.BlockSpec(a.shape, lambda bi, hp, qi: (0, 0))
    return pl.pallas_call(
        functools.partial(_diff_attn_kernel, tq=tq, lambda_init=lambda_init),
        grid=(b, n_pairs, nq),
        in_specs=[q_spec(0), q_spec(n_pairs), k_spec(0), k_spec(n_pairs),
                  pl.BlockSpec((2 * dv, s), lambda bi, hp, qi: (hp, bi)),
                  vec(lq1), vec(lk1), vec(lq2), vec(lk2), vec(subg)],
        out_specs=pl.BlockSpec((tq, 2 * dv), lambda bi, hp, qi: (bi * nq + qi, hp)),
        out_shape=jax.ShapeDtypeStruct((t, DIFF_HEADS * dv), BF16),
        scratch_shapes=[pltpu.VMEM((4, dv + ONES_ROWS, tq), F32)],
        compiler_params=_cparams(("arbitrary", "arbitrary", "arbitrary")), name="diff_attn",
    )(qa_t, qa_t, ka, ka, va_t, lq1, lk1, lq2, lk2, subg)


def _fox_attn_kernel(q_ref, k_ref, caux_ref, v_ref, o_ref, acc_ref, *, tq):
    hp = pl.program_id(1)
    qi = pl.program_id(2)
    zeros = jnp.zeros((HEAD_DIM, tq), BF16)
    row = lax.broadcasted_iota(jnp.int32, (LANES, tq), 0)
    ws = []
    for u in range(2):
        head = 2 * hp + u
        pick = jnp.where(row < BIAS_PARTS * FOX_HEADS, jnp.where(row % FOX_HEADS == head, 1.0, 0.0), 0.0).astype(BF16)
        qh = q_ref[u * HEAD_DIM:(u + 1) * HEAD_DIM, :]
        ws.append(jnp.concatenate([qh, zeros, pick] if u == 0 else [zeros, qh, pick], axis=0))
    acc_ref[...] = jnp.zeros_like(acc_ref)

    def keys(k0, tks, c):
        ks = pl.ds(pl.multiple_of(k0, tks), tks)
        return jnp.concatenate([k_ref[ks, :], caux_ref[ks, :]], axis=1)

    def values(k0, tks, c):
        v = v_ref[c * HEAD_DIM:(c + 1) * HEAD_DIM, pl.ds(pl.multiple_of(k0, tks), tks)]
        return jnp.concatenate([v, jnp.ones((ONES_ROWS, tks), BF16)], axis=0)

    _causal_sweep(qi, 2, tq, FOX_KEY_SPLIT, keys, values, ws, acc_ref)
    o = jnp.concatenate([acc_ref[u, 0:HEAD_DIM, :] / acc_ref[u, HEAD_DIM:HEAD_DIM + 1, :] for u in range(2)],
                        axis=0)
    o_ref[...] = o.T.astype(o_ref.dtype)


def _fox_attention(qf_t, kf, caux, vf_t, b, s, tq):
    t = b * s
    nq = s // tq
    pair = 2 * HEAD_DIM
    return pl.pallas_call(
        functools.partial(_fox_attn_kernel, tq=tq),
        grid=(b, FOX_HEADS // 2, nq),
        in_specs=[pl.BlockSpec((pair, tq), lambda bi, hp, qi: (hp, bi * nq + qi)),
                  pl.BlockSpec((s, pair), lambda bi, hp, qi: (bi, hp)),
                  pl.BlockSpec((s, LANES), lambda bi, hp, qi: (bi, 0)),
                  pl.BlockSpec((pair, s), lambda bi, hp, qi: (hp, bi))],
        out_specs=pl.BlockSpec((tq, pair), lambda bi, hp, qi: (bi * nq + qi, hp)),
        out_shape=jax.ShapeDtypeStruct((t, FOX_HEADS * HEAD_DIM), BF16),
        scratch_shapes=[pltpu.VMEM((2, HEAD_DIM + ONES_ROWS, tq), F32)],
        compiler_params=_cparams(("arbitrary", "arbitrary", "arbitrary")), name="fox_attn",
    )(qf_t, kf, caux, vf_t)


ROUTE_ROWS = 8
EXPERT_ROW0 = 8


def _post_kernel(x_ref, oa_ref, of_ref, gate_ref, wda, wfo, wo, g2_ref, wrt_ref, brt_ref,
                 x_o, h2_o, route_o, cnt_o, cnt_ref, *, tm):
    i = pl.program_id(0)
    d = x_ref.shape[1]
    ya = jnp.dot(oa_ref[...], wda[...], preferred_element_type=F32)
    yf = jnp.dot(of_ref[...], wfo[...], preferred_element_type=F32)
    z = gate_ref[:, :d].astype(F32) * ya + gate_ref[:, d:].astype(F32) * yf
    x = x_ref[...] + jnp.dot(z.astype(BF16), wo[...], preferred_element_type=F32)
    x_o[...] = x
    ms = jnp.mean(x * x, axis=-1, keepdims=True)
    h2 = x * lax.rsqrt(ms + EPS) * g2_ref[...]
    h2_o[...] = h2

    lg = lax.dot_general(wrt_ref[...], h2, (((1,), (1,)), ((), ())), preferred_element_type=F32,
                         precision=lax.Precision.HIGHEST) + brt_ref[...]
    sub = lax.broadcasted_iota(jnp.int32, (EXPERTS_PER_GROUP, tm), 0).astype(F32)
    gl = jnp.where(sub < N_GROUPS, lg[0:EXPERTS_PER_GROUP], NEG_BIG)
    gmax = jnp.max(gl, axis=0, keepdims=True)
    p_g = 1.0 / jnp.sum(jnp.exp(gl - gmax), axis=0, keepdims=True)
    g_idx = jnp.min(jnp.where(gl == gmax, sub, float(EXPERTS_PER_GROUP)), axis=0, keepdims=True)
    el = lg[EXPERT_ROW0:EXPERT_ROW0 + EXPERTS_PER_GROUP]
    for g in range(1, N_GROUPS):
        lo = EXPERT_ROW0 + EXPERTS_PER_GROUP * g
        el = jnp.where(g_idx == g, lg[lo:lo + EXPERTS_PER_GROUP], el)
    emax = jnp.max(el, axis=0, keepdims=True)
    esum = jnp.sum(jnp.exp(el - emax), axis=0, keepdims=True)
    i0 = jnp.min(jnp.where(el == emax, sub, float(EXPERTS_PER_GROUP)), axis=0, keepdims=True)
    el2 = jnp.where(sub == i0, NEG_BIG, el)
    e2max = jnp.max(el2, axis=0, keepdims=True)
    i1 = jnp.min(jnp.where(el2 == e2max, sub, float(EXPERTS_PER_GROUP)), axis=0, keepdims=True)
    p0 = 1.0 / esum
    p1 = jnp.exp(e2max - emax) / esum
    w0 = p_g * (p0 / (p0 + p1))
    w1 = p_g * (p1 / (p0 + p1))
    e0 = g_idx * EXPERTS_PER_GROUP + i0
    e1 = g_idx * EXPERTS_PER_GROUP + i1

    @pl.when(i == 0)
    def _():
        cnt_ref[...] = jnp.zeros_like(cnt_ref)

    erow = lax.broadcasted_iota(jnp.int32, (N_EXPERTS, tm), 0).astype(F32)
    oh0 = jnp.where(erow == e0, 1.0, 0.0)
    oh1 = jnp.where(erow == e1, 1.0, 0.0)
    both = oh0 + oh1
    r = lax.broadcasted_iota(jnp.int32, (tm, tm), 0)
    c = lax.broadcasted_iota(jnp.int32, (tm, tm), 1)
    earlier = jnp.where(r < c, 1.0, 0.0).astype(BF16)
    base = jnp.dot(both.astype(BF16), earlier, preferred_element_type=F32) + cnt_ref[:, 0:1]
    rank0 = jnp.sum(oh0 * base, axis=0, keepdims=True)
    rank1 = jnp.sum(oh1 * base, axis=0, keepdims=True)
    cnt_ref[...] = cnt_ref[...] + jnp.sum(both, axis=1, keepdims=True)
    cnt_o[...] = cnt_ref[...]

    route_o[...] = jnp.where(sub == 0, e0, jnp.where(sub == 1, e1, jnp.where(sub == 2, w0, jnp.where(
        sub == 3, w1, jnp.where(sub == 4, rank0, jnp.where(sub == 5, rank1, 0.0))))))


def _post(x2d, oa, of, gates, wda, wfo, wo, g2, wrt, brt, tm):
    t, d = x2d.shape
    row_spec = lambda w: pl.BlockSpec((tm, w), lambda i: (i, 0))
    return pl.pallas_call(
        functools.partial(_post_kernel, tm=tm),
        grid=(t // tm,),
        in_specs=[row_spec(d), row_spec(oa.shape[1]), row_spec(of.shape[1]), row_spec(gates.shape[1]),
                  _const_spec(wda.shape), _const_spec(wfo.shape), _const_spec(wo.shape), _const_spec(g2.shape),
                  _const_spec(wrt.shape), _const_spec(brt.shape)],
        out_specs=[row_spec(d), row_spec(d), pl.BlockSpec((ROUTE_ROWS, tm), lambda i: (0, i)),
                   _const_spec((N_EXPERTS, LANES))],
        out_shape=[jax.ShapeDtypeStruct((t, d), F32), jax.ShapeDtypeStruct((t, d), F32),
                   jax.ShapeDtypeStruct((ROUTE_ROWS, t), F32), jax.ShapeDtypeStruct((N_EXPERTS, LANES), F32)],
        scratch_shapes=[pltpu.VMEM((N_EXPERTS, LANES), F32)],
        compiler_params=_cparams(("arbitrary",)), name="post_attn",
    )(x2d, oa, of, gates, wda, wfo, wo, g2, wrt, brt)


def _dispatch_kernel(pos_ref, h2_ref, xs_in, xs_hbm, sem, *, tm):
    del xs_in

    def copy(r, k):
        return pltpu.make_async_copy(h2_ref.at[pl.ds(r, 1), :], xs_hbm.at[pl.ds(pos_ref[0, 0, k * tm + r], 1), :], sem)

    def issue(r, c):
        copy(r, 0).start(priority=0)
        copy(r, 1).start(priority=1)
        return c
    lax.fori_loop(0, tm, issue, 0, unroll=8)

    def wait(r, c):
        copy(r, 0).wait()
        copy(r, 1).wait()
        return c
    lax.fori_loop(0, tm, wait, 0, unroll=8)


def _dispatch(pos3d, h2, xs_zero, tm):
    t, d = h2.shape
    return pl.pallas_call(
        functools.partial(_dispatch_kernel, tm=tm),
        grid=(t // tm,),
        in_specs=[pl.BlockSpec((1, 1, 2 * tm), lambda i: (i, 0, 0), memory_space=pltpu.SMEM),
                  pl.BlockSpec((tm, d), lambda i: (i, 0)),
                  pl.BlockSpec(memory_space=pl.ANY)],
        out_specs=pl.BlockSpec(memory_space=pl.ANY),
        out_shape=jax.ShapeDtypeStruct(xs_zero.shape, xs_zero.dtype),
        scratch_shapes=[pltpu.SemaphoreType.DMA(())],
        input_output_aliases={2: 0},
        compiler_params=_cparams(("arbitrary",)), name="moe_dispatch",
    )(pos3d, h2, xs_zero)


def _moe_kernel(te_ref, tv_ref, x_ref, wg_ref, wu_ref, wd_ref, y_ref):
    i = pl.program_id(0)

    @pl.when(tv_ref[i] == 1)
    def _():
        x = x_ref[...].astype(BF16)
        a = jnp.dot(x, wg_ref[0].astype(BF16), preferred_element_type=F32)
        u = jnp.dot(x, wu_ref[0].astype(BF16), preferred_element_type=F32)
        hid = (a * jax.nn.sigmoid(a) * u).astype(BF16)
        y_ref[...] = jnp.dot(hid, wd_ref[0].astype(BF16), preferred_element_type=F32)

    @pl.when(tv_ref[i] == 0)
    def _():
        y_ref[...] = jnp.zeros_like(y_ref)


def _moe(tile_expert, tile_valid, xs, wg, wu, wd, tm):
    n_tiles = xs.shape[0] // tm
    d, f = wg.shape[1], wg.shape[2]
    grid_spec = pltpu.PrefetchScalarGridSpec(
        num_scalar_prefetch=2, grid=(n_tiles,),
        in_specs=[pl.BlockSpec((tm, d), lambda i, te, tv: (i, 0)),
                  pl.BlockSpec((1, d, f), lambda i, te, tv: (te[i], 0, 0)),
                  pl.BlockSpec((1, d, f), lambda i, te, tv: (te[i], 0, 0)),
                  pl.BlockSpec((1, f, d), lambda i, te, tv: (te[i], 0, 0))],
        out_specs=pl.BlockSpec((tm, d), lambda i, te, tv: (i, 0)))
    return pl.pallas_call(
        _moe_kernel, grid_spec=grid_spec,
        out_shape=jax.ShapeDtypeStruct((n_tiles * tm, d), F32),
        compiler_params=_cparams(("arbitrary",)), name="moe_experts",
    )(tile_expert, tile_valid, xs, wg, wu, wd)


def _combine_kernel(pos_ref, x_ref, route_ref, y_hbm, o_ref, ybuf, sem, *, tm):
    def copy(r):
        return pltpu.make_async_copy(y_hbm.at[pl.ds(pos_ref[0, 0, r], 1), :], ybuf.at[pl.ds(r, 1), :], sem)

    def issue(r, c):
        copy(r).start(priority=0)
        copy(tm + r).start(priority=1)
        return c
    lax.fori_loop(0, tm, issue, 0, unroll=8)

    def wait(r, c):
        copy(r).wait()
        return c
    lax.fori_loop(0, 2 * tm, wait, 0, unroll=8)

    d = x_ref.shape[1]
    w0 = jnp.tile(jnp.broadcast_to(route_ref[2:3, :], (LANES, tm)).T, (1, d // LANES))
    w1 = jnp.tile(jnp.broadcast_to(route_ref[3:4, :], (LANES, tm)).T, (1, d // LANES))
    o_ref[...] = x_ref[...] + w0 * ybuf[:tm, :] + w1 * ybuf[tm:, :]


def _combine(pos3d, x2d, route, y, tm):
    t, d = x2d.shape
    return pl.pallas_call(
        functools.partial(_combine_kernel, tm=tm),
        grid=(t // tm,),
        in_specs=[pl.BlockSpec((1, 1, 2 * tm), lambda i: (i, 0, 0), memory_space=pltpu.SMEM),
                  pl.BlockSpec((tm, d), lambda i: (i, 0)),
                  pl.BlockSpec((ROUTE_ROWS, tm), lambda i: (0, i)),
                  pl.BlockSpec(memory_space=pl.ANY)],
        out_specs=pl.BlockSpec((tm, d), lambda i: (i, 0)),
        out_shape=jax.ShapeDtypeStruct((t, d), F32),
        scratch_shapes=[pltpu.VMEM((2 * tm, d), F32), pltpu.SemaphoreType.DMA(())],
        compiler_params=_cparams(("arbitrary",)), name="moe_combine",
    )(pos3d, x2d, route, y)


def _dispatch_plan(route, counts, tm_e, tm_c):
    t = route.shape[1]
    counts = counts.astype(jnp.int32)
    padded = ((counts + tm_e - 1) // tm_e) * tm_e
    pad_end = jnp.cumsum(padded)
    pad_off = pad_end - padded
    n_tiles = (2 * t) // tm_e + N_EXPERTS
    tile_row0 = jnp.arange(n_tiles, dtype=jnp.int32) * tm_e
    tile_valid = (tile_row0 < pad_end[-1]).astype(jnp.int32)
    tile_expert = jnp.minimum(jnp.sum((tile_row0[:, None] >= pad_end[None, :]).astype(jnp.int32), axis=1),
                              N_EXPERTS - 1).astype(jnp.int32)
    e_ids = route[0:2].astype(jnp.int32)
    ranks = route[4:6].astype(jnp.int32)
    onehot = e_ids[:, :, None] == jnp.arange(N_EXPERTS, dtype=jnp.int32)[None, None, :]
    pos = ranks + jnp.sum(jnp.where(onehot, pad_off[None, None, :], 0), axis=-1)
    pos3d = pos.reshape(2, t // tm_c, tm_c).transpose(1, 0, 2).reshape(t // tm_c, 1, 2 * tm_c)
    return tile_expert, tile_valid, pos3d, n_tiles


def _rotary_tables(positions):
    inv_freq = ROPE_THETA ** (-jnp.arange(0, ROT_DIM, 2, dtype=F32) / ROT_DIM)
    ang = inv_freq[:, None] * positions.astype(F32).reshape(-1)[None, :]
    return jnp.cos(ang), jnp.sin(ang)


def kernel(x, positions, ln1_g, w_in, b_forget, q_norm_diff_g, k_norm_diff_g, lam_q1, lam_k1, lam_q2, lam_k2,
           subln_g, q_norm_fox_g, k_norm_fox_g, w_diff_out, w_fox_out, w_o, ln2_g, w_router_group,
           b_router_group, w_router_expert, b_router_expert, w_exp_gate, w_exp_up, w_exp_down):
    b, s, d = x.shape
    depth = w_in.shape[0]
    t = b * s
    tm = min(512, s)
    tq = min(512, s)
    tq_fox = min(1024, s)
    tm_e = 256
    tm_c = min(256, s)
    assert s % tm == 0 and s % tq == 0 and s % tq_fox == 0 and t % tm_c == 0 and (2 * t) % tm_e == 0

    qk_w = 2 * DIFF_HEADS * HEAD_DIM
    v_w = DIFF_HEADS * 2 * HEAD_DIM
    f_w = FOX_HEADS * HEAD_DIM
    assert qk_w == v_w == f_w
    bounds = [0]
    for w in (qk_w, qk_w, v_w, f_w, f_w, f_w, FOX_HEADS, 2 * d):
        bounds.append(bounds[-1] + w)

    cos_t, sin_t = _rotary_tables(positions)
    col = lambda g: g.astype(F32)[:, None]

    x2d = x.reshape(t, d)
    for l in range(depth):
        lambda_init = 0.8 - 0.6 * math.exp(-0.3 * l)
        wl = w_in[l]
        wts = [wl[:, bounds[k]:bounds[k + 1]].T.astype(BF16) for k in range(7)]
        wts[6] = jnp.pad(wts[6], ((0, 2 * SUBLANES - FOX_HEADS), (0, 0)))
        wgz = wl[:, bounds[7]:bounds[8]].astype(BF16)
        gains = [col(q_norm_diff_g[l]), col(k_norm_diff_g[l]), col(q_norm_fox_g[l]), col(k_norm_fox_g[l])]
        qa_t, ka, va_t, qf_t, kf, vf_t, caux, gates = _inproj(
            x2d, ln1_g[l][None, :], wts, wgz, gains, col(b_forget[l]), cos_t, sin_t, s, tm)

        oa = _diff_attention(qa_t, ka, va_t, col(lam_q1[l]), col(lam_k1[l]), col(lam_q2[l]), col(lam_k2[l]),
                             col(subln_g[l]), lambda_init, b, s, tq)
        of = _fox_attention(qf_t, kf, caux, vf_t, b, s, tq_fox)

        wrt = jnp.zeros((LANES, d), F32)
        wrt = wrt.at[:N_GROUPS].set(w_router_group[l].T.astype(F32))
        wrt = wrt.at[EXPERT_ROW0:EXPERT_ROW0 + N_EXPERTS].set(w_router_expert[l].T.astype(F32))
        brt = jnp.zeros((LANES, 1), F32)
        brt = brt.at[:N_GROUPS, 0].set(b_router_group[l].astype(F32))
        brt = brt.at[EXPERT_ROW0:EXPERT_ROW0 + N_EXPERTS, 0].set(b_router_expert[l].astype(F32))
        x_mid, h2, route, counts = _post(x2d, oa, of, gates,
                                         w_diff_out[l].astype(BF16), w_fox_out[l].astype(BF16),
                                         w_o[l].astype(BF16), ln2_g[l][None, :], wrt, brt, tm)

        tile_expert, tile_valid, pos3d, n_tiles = _dispatch_plan(route, counts[:, 0], tm_e, tm_c)
        xs = _dispatch(pos3d, h2, jnp.zeros((n_tiles * tm_e, d), F32), tm_c)
        f = w_exp_gate.shape[-1]
        y = _moe(tile_expert + l * N_EXPERTS, tile_valid, xs,
                 w_exp_gate.reshape(depth * N_EXPERTS, d, f), w_exp_up.reshape(depth * N_EXPERTS, d, f),
                 w_exp_down.reshape(depth * N_EXPERTS, f, d), tm_e)
        x2d = _combine(pos3d, x_mid, route, y, tm_c)
    return x2d.reshape(b, s, d)
```

```python
import functools
import math

import jax
import jax.numpy as jnp
from jax import lax
from jax.experimental import pallas as pl
from jax.experimental.pallas import tpu as pltpu

F32 = jnp.float32
BF16 = jnp.bfloat16

HEAD_DIM = 64
DIFF_HEADS = 4
FOX_HEADS = 8
ROT_DIM = HEAD_DIM // 4
ROPE_THETA = 500000.0
N_GROUPS = 4
EXPERTS_PER_GROUP = 8
N_EXPERTS = N_GROUPS * EXPERTS_PER_GROUP
EPS = 1e-6
LANES = 128
SUBLANES = 8
NEG_BIG = -1e30
LOG2E = math.log2(math.e)
Q_SCALE = HEAD_DIM ** -0.5 * LOG2E
BIAS_PARTS = 3

VMEM_LIMIT = 56 * 1024 * 1024


def _cparams(sem):
    return pltpu.CompilerParams(dimension_semantics=sem, vmem_limit_bytes=VMEM_LIMIT)


def _const_spec(shape):
    nd = len(shape)
    return pl.BlockSpec(shape, lambda *_: (0,) * nd)


def _nt_dot(a, b):
    return lax.dot_general(a, b, (((1,), (1,)), ((), ())), preferred_element_type=F32)


def _split3(a):
    p1 = a.astype(BF16)
    r1 = a - p1.astype(F32)
    p2 = r1.astype(BF16)
    p3 = (r1 - p2.astype(F32)).astype(BF16)
    return p1, p2, p3


def _inproj_kernel(x_ref, g1_ref, wqa, wka, wva, wqf, wkf, wvf, wfz, wgz, gqa, gka, gqf, gkf, bf_ref,
                   cos_ref, sin_ref,
                   qa_o, ka_o, va_o, qf_o, kf_o, vf_o, caux_o, gate_o, carry_ref, *, tiles_per_seq, tm):
    i = pl.program_id(0)
    x = x_ref[...]
    ms = jnp.mean(x * x, axis=-1, keepdims=True)
    h = (x * lax.rsqrt(ms + EPS) * g1_ref[...]).astype(BF16)
    cos = cos_ref[...]
    sin = sin_ref[...]
    half = ROT_DIM // 2

    def proj_t(w_ref):
        return _nt_dot(w_ref[...], h)

    def head_norm(y, g_ref, rotate, scale):
        g = jnp.broadcast_to(g_ref[...], (HEAD_DIM, tm))
        outs = []
        for j in range(y.shape[0] // HEAD_DIM):
            yj = y[j * HEAD_DIM:(j + 1) * HEAD_DIM]
            hms = jnp.mean(yj * yj, axis=0, keepdims=True)
            yn = yj * lax.rsqrt(hms + EPS) * g
            if rotate:
                t1, t2 = yn[0:half], yn[half:ROT_DIM]
                yn = jnp.concatenate([t1 * cos - t2 * sin, t2 * cos + t1 * sin, yn[ROT_DIM:]], axis=0)
            outs.append(yn * scale if scale != 1.0 else yn)
        return jnp.concatenate(outs, axis=0)

    qa_o[...] = head_norm(proj_t(wqa), gqa, True, Q_SCALE).astype(BF16)
    ka_o[...] = head_norm(proj_t(wka), gka, True, 1.0).T.astype(BF16)
    va_o[...] = proj_t(wva).astype(BF16)
    qf_o[...] = head_norm(proj_t(wqf), gqf, False, Q_SCALE).astype(BF16)
    kf_o[...] = head_norm(proj_t(wkf), gkf, False, 1.0).T.astype(BF16)
    vf_o[...] = proj_t(wvf).astype(BF16)

    z = proj_t(wfz)[0:FOX_HEADS] + bf_ref[...]
    ls = jnp.minimum(z, 0.0) - jnp.log(1.0 + jnp.exp(-jnp.abs(z)))
    r = lax.broadcasted_iota(jnp.int32, (tm, tm), 0)
    c = lax.broadcasted_iota(jnp.int32, (tm, tm), 1)
    upto = jnp.where(r <= c, 1.0, 0.0).astype(BF16)
    csum = sum(jnp.dot(p, upto, preferred_element_type=F32) for p in _split3(ls))

    @pl.when(i % tiles_per_seq == 0)
    def _():
        carry_ref[...] = jnp.zeros_like(carry_ref)

    cum = csum + carry_ref[:, 0:1]
    carry_ref[...] = jnp.broadcast_to(cum[:, tm - 1:tm], carry_ref.shape)
    parts = [p.astype(F32) for p in _split3(-LOG2E * cum)]
    pad = jnp.zeros((LANES - BIAS_PARTS * FOX_HEADS, tm), F32)
    caux_o[...] = jnp.concatenate(parts + [pad], axis=0).T.astype(BF16)

    gw = gate_o.shape[1]
    step = 512
    for cb in range(gw // step):
        zz = jnp.dot(h, wgz[:, cb * step:(cb + 1) * step], preferred_element_type=F32)
        gate_o[:, cb * step:(cb + 1) * step] = jax.nn.sigmoid(zz).astype(BF16)


def _inproj(x2d, g1, wts, wgz, gains, bfz, cos_t, sin_t, seq, tm):
    t, d = x2d.shape
    row_spec = lambda w: pl.BlockSpec((tm, w), lambda i: (i, 0))
    col_spec = lambda r: pl.BlockSpec((r, tm), lambda i: (0, i))
    in_specs = ([row_spec(d), _const_spec(g1.shape)] + [_const_spec(w.shape) for w in wts] + [_const_spec(wgz.shape)]
                + [_const_spec(g.shape) for g in gains] + [_const_spec(bfz.shape), col_spec(cos_t.shape[0]),
                                                           col_spec(sin_t.shape[0])])
    fw = wts[0].shape[0]
    tr = jax.ShapeDtypeStruct((fw, t), BF16)
    rm = jax.ShapeDtypeStruct((t, fw), BF16)
    out_shape = [tr, rm, tr, tr, rm, tr, jax.ShapeDtypeStruct((t, LANES), BF16),
                 jax.ShapeDtypeStruct((t, wgz.shape[1]), BF16)]
    out_specs = [col_spec(fw), row_spec(fw), col_spec(fw), col_spec(fw), row_spec(fw), col_spec(fw),
                 row_spec(LANES), row_spec(wgz.shape[1])]
    return pl.pallas_call(
        functools.partial(_inproj_kernel, tiles_per_seq=seq // tm, tm=tm),
        grid=(t // tm,), in_specs=in_specs, out_specs=out_specs, out_shape=out_shape,
        scratch_shapes=[pltpu.VMEM((FOX_HEADS, LANES), F32)],
        compiler_params=_cparams(("arbitrary",)), name="inproj",
    )(x2d, g1, *wts, wgz, *gains, bfz, cos_t, sin_t)


DIFF_KEY_SPLIT = 2
FOX_KEY_SPLIT = 4
ONES_ROWS = 2 * SUBLANES


def _masked_scores(kx, w, masked):
    s = jnp.dot(kx, w, preferred_element_type=F32)
    if masked:
        key = lax.broadcasted_iota(jnp.int32, s.shape, 0)
        qry = lax.broadcasted_iota(jnp.int32, s.shape, 1)
        s = jnp.where(key <= qry, s, NEG_BIG)
    return s


def _absorb(s, v_ext, m, acc_ref, c, q_lo):
    nq = s.shape[1]
    m_new = jnp.maximum(m, jnp.max(s, axis=0, keepdims=True))
    alpha = jnp.exp2(m - m_new)
    p = jnp.exp2((s - m_new).astype(BF16))
    acc_ref[c, :, q_lo:q_lo + nq] = (alpha * acc_ref[c, :, q_lo:q_lo + nq]
                                     + jnp.dot(v_ext, p, preferred_element_type=F32))
    return m_new


def _causal_sweep(qi, n_chains, tq, key_split, keys, values, ws, acc_ref):
    half = tq // key_split
    chains = range(n_chains)

    def block(k0, ms, masked):
        subs = tuple((k0 + i * half, i * half if masked else 0) for i in range(key_split))
        scores = [[_masked_scores(keys(ks, half, c), ws[c][:, q_lo:], masked) for c in chains]
                  for ks, q_lo in subs]
        for (ks, q_lo), ss in zip(subs, scores):
            new = [_absorb(ss[c], values(ks, half, c), ms[c][:, q_lo:], acc_ref, c, q_lo) for c in chains]
            ms = tuple(jnp.concatenate([ms[c][:, :q_lo], new[c]], axis=1) if q_lo else new[c] for c in chains)
        return ms

    init = tuple(jnp.full((1, tq), NEG_BIG, F32) for _ in chains)
    ms = lax.fori_loop(0, qi, lambda j, ms: block(pl.multiple_of(j * tq, tq), ms, False), init)
    block(pl.multiple_of(qi * tq, tq), ms, True)


def _diff_attn_kernel(q1_ref, q2_ref, k1_ref, k2_ref, v_ref, lq1, lk1, lq2, lk2, sg_ref, o_ref, acc_ref,
                      *, tq, lambda_init):
    qi = pl.program_id(2)
    dv = 2 * HEAD_DIM
    zeros = jnp.zeros((HEAD_DIM, tq), BF16)
    ws = []
    for q_ref in (q1_ref, q2_ref):
        ws.append(jnp.concatenate([q_ref[0:HEAD_DIM, :], zeros], axis=0))
        ws.append(jnp.concatenate([zeros, q_ref[HEAD_DIM:2 * HEAD_DIM, :]], axis=0))
    acc_ref[...] = jnp.zeros_like(acc_ref)

    def keys(k0, tks, c):
        return (k1_ref, k1_ref, k2_ref, k2_ref)[c][pl.ds(pl.multiple_of(k0, tks), tks), :]

    def values(k0, tks, c):
        u = c % 2
        v = v_ref[u * dv:(u + 1) * dv, pl.ds(pl.multiple_of(k0, tks), tks)]
        return jnp.concatenate([v, jnp.ones((ONES_ROWS, tks), BF16)], axis=0)

    _causal_sweep(qi, 4, tq, DIFF_KEY_SPLIT, keys, values, ws, acc_ref)
    lam = (jnp.exp(jnp.sum(lq1[...] * lk1[...], axis=0, keepdims=True))
           - jnp.exp(jnp.sum(lq2[...] * lk2[...], axis=0, keepdims=True)) + lambda_init)
    sg = sg_ref[...] * (1.0 - lambda_init)
    outs = []
    for c1, c2 in ((0, 2), (1, 3)):
        o = (acc_ref[c1, 0:dv, :] / acc_ref[c1, dv:dv + 1, :]
             - lam * (acc_ref[c2, 0:dv, :] / acc_ref[c2, dv:dv + 1, :]))
        oms = jnp.mean(o * o, axis=0, keepdims=True)
        outs.append((o * lax.rsqrt(oms + EPS) * sg).T)
    o_ref[...] = jnp.concatenate(outs, axis=1).astype(o_ref.dtype)


def _diff_attention(qa_t, ka, va_t, lq1, lk1, lq2, lk2, subg, lambda_init, b, s, tq):
    t = b * s
    nq = s // tq
    pair = 2 * HEAD_DIM
    dv = 2 * HEAD_DIM
    n_pairs = DIFF_HEADS // 2
    q_spec = lambda off: pl.BlockSpec((pair, tq), lambda bi, hp, qi: (hp + off, bi * nq + qi))
    k_spec = lambda off: pl.BlockSpec((s, pair), lambda bi, hp, qi: (bi, hp + off))
    vec = lambda a: pl.BlockSpec(a.shape, lambda bi, hp, qi: (0, 0))
    return pl.pallas_call(
        functools.partial(_diff_attn_kernel, tq=tq, lambda_init=lambda_init),
        grid=(b, n_pairs, nq),
        in_specs=[q_spec(0), q_spec(n_pairs), k_spec(0), k_spec(n_pairs),
                  pl.BlockSpec((2 * dv, s), lambda bi, hp, qi: (hp, bi)),
                  vec(lq1), vec(lk1), vec(lq2), vec(lk2), vec(subg)],
        out_specs=pl.BlockSpec((tq, 2 * dv), lambda bi, hp, qi: (bi * nq + qi, hp)),
        out_shape=jax.ShapeDtypeStruct((t, DIFF_HEADS * dv), BF16),
        scratch_shapes=[pltpu.VMEM((4, dv + ONES_ROWS, tq), F32)],
        compiler_params=_cparams(("arbitrary", "arbitrary", "arbitrary")), name="diff_attn",
    )(qa_t, qa_t, ka, ka, va_t, lq1, lk1, lq2, lk2, subg)


def _fox_attn_kernel(q_ref, k_ref, caux_ref, v_ref, o_ref, acc_ref, *, tq):
    hp = pl.program_id(1)
    qi = pl.program_id(2)
    zeros = jnp.zeros((HEAD_DIM, tq), BF16)
    row = lax.broadcasted_iota(jnp.int32, (LANES, tq), 0)
    ws = []
    for u in range(2):
        head = 2 * hp + u
        pick = jnp.where(row < BIAS_PARTS * FOX_HEADS, jnp.where(row % FOX_HEADS == head, 1.0, 0.0), 0.0).astype(BF16)
        qh = q_ref[u * HEAD_DIM:(u + 1) * HEAD_DIM, :]
        ws.append(jnp.concatenate([qh, zeros, pick] if u == 0 else [zeros, qh, pick], axis=0))
    acc_ref[...] = jnp.zeros_like(acc_ref)

    def keys(k0, tks, c):
        ks = pl.ds(pl.multiple_of(k0, tks), tks)
        return jnp.concatenate([k_ref[ks, :], caux_ref[ks, :]], axis=1)

    def values(k0, tks, c):
        v = v_ref[c * HEAD_DIM:(c + 1) * HEAD_DIM, pl.ds(pl.multiple_of(k0, tks), tks)]
        return jnp.concatenate([v, jnp.ones((ONES_ROWS, tks), BF16)], axis=0)

    _causal_sweep(qi, 2, tq, FOX_KEY_SPLIT, keys, values, ws, acc_ref)
    o = jnp.concatenate([acc_ref[u, 0:HEAD_DIM, :] / acc_ref[u, HEAD_DIM:HEAD_DIM + 1, :] for u in range(2)],
                        axis=0)
    o_ref[...] = o.T.astype(o_ref.dtype)


def _fox_attention(qf_t, kf, caux, vf_t, b, s, tq):
    t = b * s
    nq = s // tq
    pair = 2 * HEAD_DIM
    return pl.pallas_call(
        functools.partial(_fox_attn_kernel, tq=tq),
        grid=(b, FOX_HEADS // 2, nq),
        in_specs=[pl.BlockSpec((pair, tq), lambda bi, hp, qi: (hp, bi * nq + qi)),
                  pl.BlockSpec((s, pair), lambda bi, hp, qi: (bi, hp)),
                  pl.BlockSpec((s, LANES), lambda bi, hp, qi: (bi, 0)),
                  pl.BlockSpec((pair, s), lambda bi, hp, qi: (hp, bi))],
        out_specs=pl.BlockSpec((tq, pair), lambda bi, hp, qi: (bi * nq + qi, hp)),
        out_shape=jax.ShapeDtypeStruct((t, FOX_HEADS * HEAD_DIM), BF16),
        scratch_shapes=[pltpu.VMEM((2, HEAD_DIM + ONES_ROWS, tq), F32)],
        compiler_params=_cparams(("arbitrary", "arbitrary", "arbitrary")), name="fox_attn",
    )(qf_t, kf, caux, vf_t)


def _tile_rows_shape(n, d):
    assert d == SUBLANES * LANES
    return (n * SUBLANES, LANES)


def _tile_row(ref, r):
    return ref.at[pl.ds(pl.multiple_of(r * SUBLANES, SUBLANES), SUBLANES), :]


def _store_tile_rows(ref, a):
    for j in range(SUBLANES):
        ref[pl.ds(j, a.shape[0], stride=SUBLANES), :] = a[:, j * LANES:(j + 1) * LANES]


def _load_tile_rows(ref, lo, n):
    return jnp.concatenate([ref[pl.ds(lo * SUBLANES + j, n, stride=SUBLANES), :] for j in range(SUBLANES)], axis=1)


ROUTE_ROWS = 8
EXPERT_ROW0 = 8


def _post_kernel(x_ref, oa_ref, of_ref, gate_ref, wda, wfo, wo, g2_ref, wrt_ref, brt_ref,
                 x_o, h2_o, route_o, cnt_o, cnt_ref, *, tm):
    i = pl.program_id(0)
    d = x_ref.shape[1]
    ya = jnp.dot(oa_ref[...], wda[...], preferred_element_type=F32)
    yf = jnp.dot(of_ref[...], wfo[...], preferred_element_type=F32)
    z = gate_ref[:, :d].astype(F32) * ya + gate_ref[:, d:].astype(F32) * yf
    x = x_ref[...] + jnp.dot(z.astype(BF16), wo[...], preferred_element_type=F32)
    x_o[...] = x
    ms = jnp.mean(x * x, axis=-1, keepdims=True)
    h2 = x * lax.rsqrt(ms + EPS) * g2_ref[...]
    _store_tile_rows(h2_o, h2)

    lg = lax.dot_general(wrt_ref[...], h2, (((1,), (1,)), ((), ())), preferred_element_type=F32,
                         precision=lax.Precision.HIGHEST) + brt_ref[...]
    sub = lax.broadcasted_iota(jnp.int32, (EXPERTS_PER_GROUP, tm), 0).astype(F32)
    gl = jnp.where(sub < N_GROUPS, lg[0:EXPERTS_PER_GROUP], NEG_BIG)
    gmax = jnp.max(gl, axis=0, keepdims=True)
    p_g = 1.0 / jnp.sum(jnp.exp(gl - gmax), axis=0, keepdims=True)
    g_idx = jnp.min(jnp.where(gl == gmax, sub, float(EXPERTS_PER_GROUP)), axis=0, keepdims=True)
    el = lg[EXPERT_ROW0:EXPERT_ROW0 + EXPERTS_PER_GROUP]
    for g in range(1, N_GROUPS):
        lo = EXPERT_ROW0 + EXPERTS_PER_GROUP * g
        el = jnp.where(g_idx == g, lg[lo:lo + EXPERTS_PER_GROUP], el)
    emax = jnp.max(el, axis=0, keepdims=True)
    esum = jnp.sum(jnp.exp(el - emax), axis=0, keepdims=True)
    i0 = jnp.min(jnp.where(el == emax, sub, float(EXPERTS_PER_GROUP)), axis=0, keepdims=True)
    el2 = jnp.where(sub == i0, NEG_BIG, el)
    e2max = jnp.max(el2, axis=0, keepdims=True)
    i1 = jnp.min(jnp.where(el2 == e2max, sub, float(EXPERTS_PER_GROUP)), axis=0, keepdims=True)
    p0 = 1.0 / esum
    p1 = jnp.exp(e2max - emax) / esum
    w0 = p_g * (p0 / (p0 + p1))
    w1 = p_g * (p1 / (p0 + p1))
    e0 = g_idx * EXPERTS_PER_GROUP + i0
    e1 = g_idx * EXPERTS_PER_GROUP + i1

    @pl.when(i == 0)
    def _():
        cnt_ref[...] = jnp.zeros_like(cnt_ref)

    erow = lax.broadcasted_iota(jnp.int32, (N_EXPERTS, tm), 0).astype(F32)
    oh0 = jnp.where(erow == e0, 1.0, 0.0)
    oh1 = jnp.where(erow == e1, 1.0, 0.0)
    both = oh0 + oh1
    r = lax.broadcasted_iota(jnp.int32, (tm, tm), 0)
    c = lax.broadcasted_iota(jnp.int32, (tm, tm), 1)
    earlier = jnp.where(r < c, 1.0, 0.0).astype(BF16)
    base = jnp.dot(both.astype(BF16), earlier, preferred_element_type=F32) + cnt_ref[:, 0:1]
    rank0 = jnp.sum(oh0 * base, axis=0, keepdims=True)
    rank1 = jnp.sum(oh1 * base, axis=0, keepdims=True)
    cnt_ref[...] = cnt_ref[...] + jnp.sum(both, axis=1, keepdims=True)
    cnt_o[...] = cnt_ref[...]

    route_o[...] = jnp.where(sub == 0, e0, jnp.where(sub == 1, e1, jnp.where(sub == 2, w0, jnp.where(
        sub == 3, w1, jnp.where(sub == 4, rank0, jnp.where(sub == 5, rank1, 0.0))))))


def _post(x2d, oa, of, gates, wda, wfo, wo, g2, wrt, brt, tm):
    t, d = x2d.shape
    row_spec = lambda w: pl.BlockSpec((tm, w), lambda i: (i, 0))
    return pl.pallas_call(
        functools.partial(_post_kernel, tm=tm),
        grid=(t // tm,),
        in_specs=[row_spec(d), row_spec(oa.shape[1]), row_spec(of.shape[1]), row_spec(gates.shape[1]),
                  _const_spec(wda.shape), _const_spec(wfo.shape), _const_spec(wo.shape), _const_spec(g2.shape),
                  _const_spec(wrt.shape), _const_spec(brt.shape)],
        out_specs=[row_spec(d), pl.BlockSpec(_tile_rows_shape(tm, d), lambda i: (i, 0)),
                   pl.BlockSpec((ROUTE_ROWS, tm), lambda i: (0, i)), _const_spec((N_EXPERTS, LANES))],
        out_shape=[jax.ShapeDtypeStruct((t, d), F32), jax.ShapeDtypeStruct(_tile_rows_shape(t, d), F32),
                   jax.ShapeDtypeStruct((ROUTE_ROWS, t), F32), jax.ShapeDtypeStruct((N_EXPERTS, LANES), F32)],
        scratch_shapes=[pltpu.VMEM((N_EXPERTS, LANES), F32)],
        compiler_params=_cparams(("arbitrary",)), name="post_attn",
    )(x2d, oa, of, gates, wda, wfo, wo, g2, wrt, brt)


def _dispatch_kernel(pos_ref, h2_ref, xs_in, xs_hbm, sem, *, tm):
    del xs_in

    def copy(r, k):
        return pltpu.make_async_copy(_tile_row(h2_ref, r), _tile_row(xs_hbm, pos_ref[0, 0, k * tm + r]), sem)

    def issue(r, c):
        copy(r, 0).start(priority=0)
        copy(r, 1).start(priority=1)
        return c
    lax.fori_loop(0, tm, issue, 0, unroll=8)

    def wait(r, c):
        copy(r, 0).wait()
        copy(r, 1).wait()
        return c
    lax.fori_loop(0, tm, wait, 0, unroll=8)


def _dispatch(pos3d, h2, xs_zero, tm):
    t = h2.shape[0] // SUBLANES
    return pl.pallas_call(
        functools.partial(_dispatch_kernel, tm=tm),
        grid=(t // tm,),
        in_specs=[pl.BlockSpec((1, 1, 2 * tm), lambda i: (i, 0, 0), memory_space=pltpu.SMEM),
                  pl.BlockSpec((tm * SUBLANES, LANES), lambda i: (i, 0)),
                  pl.BlockSpec(memory_space=pl.ANY)],
        out_specs=pl.BlockSpec(memory_space=pl.ANY),
        out_shape=jax.ShapeDtypeStruct(xs_zero.shape, xs_zero.dtype),
        scratch_shapes=[pltpu.SemaphoreType.DMA(())],
        input_output_aliases={2: 0},
        compiler_params=_cparams(("arbitrary",)), name="moe_dispatch",
    )(pos3d, h2, xs_zero)


def _moe_kernel(te_ref, tv_ref, x_ref, wg_ref, wu_ref, wd_ref, y_ref):
    i = pl.program_id(0)

    @pl.when(tv_ref[i] == 1)
    def _():
        x = _load_tile_rows(x_ref, 0, x_ref.shape[0] // SUBLANES).astype(BF16)
        a = jnp.dot(x, wg_ref[0].astype(BF16), preferred_element_type=F32)
        u = jnp.dot(x, wu_ref[0].astype(BF16), preferred_element_type=F32)
        hid = (a * jax.nn.sigmoid(a) * u).astype(BF16)
        _store_tile_rows(y_ref, jnp.dot(hid, wd_ref[0].astype(BF16), preferred_element_type=F32))

    @pl.when(tv_ref[i] == 0)
    def _():
        y_ref[...] = jnp.zeros_like(y_ref)


def _moe(tile_expert, tile_valid, xs, wg, wu, wd, tm):
    n_tiles = xs.shape[0] // (tm * SUBLANES)
    d, f = wg.shape[1], wg.shape[2]
    rows_spec = pl.BlockSpec(_tile_rows_shape(tm, d), lambda i, te, tv: (i, 0))
    grid_spec = pltpu.PrefetchScalarGridSpec(
        num_scalar_prefetch=2, grid=(n_tiles,),
        in_specs=[rows_spec,
                  pl.BlockSpec((1, d, f), lambda i, te, tv: (te[i], 0, 0)),
                  pl.BlockSpec((1, d, f), lambda i, te, tv: (te[i], 0, 0)),
                  pl.BlockSpec((1, f, d), lambda i, te, tv: (te[i], 0, 0))],
        out_specs=rows_spec)
    return pl.pallas_call(
        _moe_kernel, grid_spec=grid_spec,
        out_shape=jax.ShapeDtypeStruct(_tile_rows_shape(n_tiles * tm, d), F32),
        compiler_params=_cparams(("arbitrary",)), name="moe_experts",
    )(tile_expert, tile_valid, xs, wg, wu, wd)


def _combine_kernel(pos_ref, x_ref, route_ref, y_hbm, o_ref, ybuf, sem, *, tm):
    def copy(r):
        return pltpu.make_async_copy(_tile_row(y_hbm, pos_ref[0, 0, r]), _tile_row(ybuf, r), sem)

    def issue(r, c):
        copy(r).start(priority=0)
        copy(tm + r).start(priority=1)
        return c
    lax.fori_loop(0, tm, issue, 0, unroll=8)

    def wait(r, c):
        copy(r).wait()
        return c
    lax.fori_loop(0, 2 * tm, wait, 0, unroll=8)

    d = x_ref.shape[1]
    w0 = jnp.tile(jnp.broadcast_to(route_ref[2:3, :], (LANES, tm)).T, (1, d // LANES))
    w1 = jnp.tile(jnp.broadcast_to(route_ref[3:4, :], (LANES, tm)).T, (1, d // LANES))
    o_ref[...] = x_ref[...] + w0 * _load_tile_rows(ybuf, 0, tm) + w1 * _load_tile_rows(ybuf, tm, tm)


def _combine(pos3d, x2d, route, y, tm):
    t, d = x2d.shape
    return pl.pallas_call(
        functools.partial(_combine_kernel, tm=tm),
        grid=(t // tm,),
        in_specs=[pl.BlockSpec((1, 1, 2 * tm), lambda i: (i, 0, 0), memory_space=pltpu.SMEM),
                  pl.BlockSpec((tm, d), lambda i: (i, 0)),
                  pl.BlockSpec((ROUTE_ROWS, tm), lambda i: (0, i)),
                  pl.BlockSpec(memory_space=pl.ANY)],
        out_specs=pl.BlockSpec((tm, d), lambda i: (i, 0)),
        out_shape=jax.ShapeDtypeStruct((t, d), F32),
        scratch_shapes=[pltpu.VMEM(_tile_rows_shape(2 * tm, d), F32), pltpu.SemaphoreType.DMA(())],
        compiler_params=_cparams(("arbitrary",)), name="moe_combine",
    )(pos3d, x2d, route, y)


def _dispatch_plan(route, counts, tm_e, tm_c):
    t = route.shape[1]
    counts = counts.astype(jnp.int32)
    padded = ((counts + tm_e - 1) // tm_e) * tm_e
    pad_end = jnp.cumsum(padded)
    pad_off = pad_end - padded
    n_tiles = (2 * t) // tm_e + N_EXPERTS
    tile_row0 = jnp.arange(n_tiles, dtype=jnp.int32) * tm_e
    tile_valid = (tile_row0 < pad_end[-1]).astype(jnp.int32)
    tile_expert = jnp.minimum(jnp.sum((tile_row0[:, None] >= pad_end[None, :]).astype(jnp.int32), axis=1),
                              N_EXPERTS - 1).astype(jnp.int32)
    e_ids = route[0:2].astype(jnp.int32)
    ranks = route[4:6].astype(jnp.int32)
    onehot = e_ids[:, :, None] == jnp.arange(N_EXPERTS, dtype=jnp.int32)[None, None, :]
    pos = ranks + jnp.sum(jnp.where(onehot, pad_off[None, None, :], 0), axis=-1)
    pos3d = pos.reshape(2, t // tm_c, tm_c).transpose(1, 0, 2).reshape(t // tm_c, 1, 2 * tm_c)
    return tile_expert, tile_valid, pos3d, n_tiles


def _rotary_tables(positions):
    inv_freq = ROPE_THETA ** (-jnp.arange(0, ROT_DIM, 2, dtype=F32) / ROT_DIM)
    ang = inv_freq[:, None] * positions.astype(F32).reshape(-1)[None, :]
    return jnp.cos(ang), jnp.sin(ang)


def kernel(x, positions, ln1_g, w_in, b_forget, q_norm_diff_g, k_norm_diff_g, lam_q1, lam_k1, lam_q2, lam_k2,
           subln_g, q_norm_fox_g, k_norm_fox_g, w_diff_out, w_fox_out, w_o, ln2_g, w_router_group,
           b_router_group, w_router_expert, b_router_expert, w_exp_gate, w_exp_up, w_exp_down):
    b, s, d = x.shape
    depth = w_in.shape[0]
    t = b * s
    tm = min(512, s)
    tq = min(512, s)
    tq_fox = min(1024, s)
    tm_e = 256
    tm_c = min(256, s)
    assert s % tm == 0 and s % tq == 0 and s % tq_fox == 0 and t % tm_c == 0 and (2 * t) % tm_e == 0

    qk_w = 2 * DIFF_HEADS * HEAD_DIM
    v_w = DIFF_HEADS * 2 * HEAD_DIM
    f_w = FOX_HEADS * HEAD_DIM
    assert qk_w == v_w == f_w
    bounds = [0]
    for w in (qk_w, qk_w, v_w, f_w, f_w, f_w, FOX_HEADS, 2 * d):
        bounds.append(bounds[-1] + w)

    cos_t, sin_t = _rotary_tables(positions)
    col = lambda g: g.astype(F32)[:, None]

    x2d = x.reshape(t, d)
    for l in range(depth):
        lambda_init = 0.8 - 0.6 * math.exp(-0.3 * l)
        wl = w_in[l]
        wts = [wl[:, bounds[k]:bounds[k + 1]].T.astype(BF16) for k in range(7)]
        wts[6] = jnp.pad(wts[6], ((0, 2 * SUBLANES - FOX_HEADS), (0, 0)))
        wgz = wl[:, bounds[7]:bounds[8]].astype(BF16)
        gains = [col(q_norm_diff_g[l]), col(k_norm_diff_g[l]), col(q_norm_fox_g[l]), col(k_norm_fox_g[l])]
        qa_t, ka, va_t, qf_t, kf, vf_t, caux, gates = _inproj(
            x2d, ln1_g[l][None, :], wts, wgz, gains, col(b_forget[l]), cos_t, sin_t, s, tm)

        oa = _diff_attention(qa_t, ka, va_t, col(lam_q1[l]), col(lam_k1[l]), col(lam_q2[l]), col(lam_k2[l]),
                             col(subln_g[l]), lambda_init, b, s, tq)
        of = _fox_attention(qf_t, kf, caux, vf_t, b, s, tq_fox)

        wrt = jnp.zeros((LANES, d), F32)
        wrt = wrt.at[:N_GROUPS].set(w_router_group[l].T.astype(F32))
        wrt = wrt.at[EXPERT_ROW0:EXPERT_ROW0 + N_EXPERTS].set(w_router_expert[l].T.astype(F32))
        brt = jnp.zeros((LANES, 1), F32)
        brt = brt.at[:N_GROUPS, 0].set(b_router_group[l].astype(F32))
        brt = brt.at[EXPERT_ROW0:EXPERT_ROW0 + N_EXPERTS, 0].set(b_router_expert[l].astype(F32))
        x_mid, h2, route, counts = _post(x2d, oa, of, gates,
                                         w_diff_out[l].astype(BF16), w_fox_out[l].astype(BF16),
                                         w_o[l].astype(BF16), ln2_g[l][None, :], wrt, brt, tm)

        tile_expert, tile_valid, pos3d, n_tiles = _dispatch_plan(route, counts[:, 0], tm_e, tm_c)
        xs = _dispatch(pos3d, h2, jnp.zeros(_tile_rows_shape(n_tiles * tm_e, d), F32), tm_c)
        f = w_exp_gate.shape[-1]
        y = _moe(tile_expert + l * N_EXPERTS, tile_valid, xs,
                 w_exp_gate.reshape(depth * N_EXPERTS, d, f), w_exp_up.reshape(depth * N_EXPERTS, d, f),
                 w_exp_down.reshape(depth * N_EXPERTS, f, d), tm_e)
        x2d = _combine(pos3d, x_mid, route, y, tm_c)
    return x2d.reshape(b, s, d)
```

```python
import functools
import math

import jax
import jax.numpy as jnp
from jax import lax
from jax.experimental import pallas as pl
from jax.experimental.pallas import tpu as pltpu

F32 = jnp.float32
BF16 = jnp.bfloat16

HEAD_DIM = 64
DIFF_HEADS = 4
FOX_HEADS = 8
ROT_DIM = HEAD_DIM // 4
ROPE_THETA = 500000.0
N_GROUPS = 4
EXPERTS_PER_GROUP = 8
N_EXPERTS = N_GROUPS * EXPERTS_PER_GROUP
EPS = 1e-6
LANES = 128
SUBLANES = 8
NEG_BIG = -1e30
LOG2E = math.log2(math.e)
Q_SCALE = HEAD_DIM ** -0.5 * LOG2E
BIAS_PARTS = 3

VMEM_LIMIT = 56 * 1024 * 1024


def _cparams(sem):
    return pltpu.CompilerParams(dimension_semantics=sem, vmem_limit_bytes=VMEM_LIMIT)


def _const_spec(shape):
    nd = len(shape)
    return pl.BlockSpec(shape, lambda *_: (0,) * nd)


def _nt_dot(a, b):
    return lax.dot_general(a, b, (((1,), (1,)), ((), ())), preferred_element_type=F32)


def _split3(a):
    p1 = a.astype(BF16)
    r1 = a - p1.astype(F32)
    p2 = r1.astype(BF16)
    p3 = (r1 - p2.astype(F32)).astype(BF16)
    return p1, p2, p3


def _inproj_kernel(x_ref, g1_ref, wqa, wka, wva, wqf, wkf, wvf, wfz, wgz, gqa, gka, gqf, gkf, bf_ref,
                   cos_ref, sin_ref,
                   qa_o, ka_o, va_o, qf_o, kf_o, vf_o, caux_o, gate_o, carry_ref, *, tiles_per_seq, tm):
    i = pl.program_id(0)
    x = x_ref[...]
    ms = jnp.mean(x * x, axis=-1, keepdims=True)
    h = (x * lax.rsqrt(ms + EPS) * g1_ref[...]).astype(BF16)
    cos = cos_ref[...]
    sin = sin_ref[...]
    half = ROT_DIM // 2

    def proj_t(w_ref):
        return _nt_dot(w_ref[...], h)

    def head_norm(y, g_ref, rotate, scale):
        g = jnp.broadcast_to(g_ref[...], (HEAD_DIM, tm))
        outs = []
        for j in range(y.shape[0] // HEAD_DIM):
            yj = y[j * HEAD_DIM:(j + 1) * HEAD_DIM]
            hms = jnp.mean(yj * yj, axis=0, keepdims=True)
            yn = yj * lax.rsqrt(hms + EPS) * g
            if rotate:
                t1, t2 = yn[0:half], yn[half:ROT_DIM]
                yn = jnp.concatenate([t1 * cos - t2 * sin, t2 * cos + t1 * sin, yn[ROT_DIM:]], axis=0)
            outs.append(yn * scale if scale != 1.0 else yn)
        return jnp.concatenate(outs, axis=0)

    qa_o[...] = head_norm(proj_t(wqa), gqa, True, Q_SCALE).astype(BF16)
    ka_o[...] = head_norm(proj_t(wka), gka, True, 1.0).T.astype(BF16)
    va_o[...] = proj_t(wva).astype(BF16)
    qf_o[...] = head_norm(proj_t(wqf), gqf, False, Q_SCALE).astype(BF16)
    kf_o[...] = head_norm(proj_t(wkf), gkf, False, 1.0).T.astype(BF16)
    vf_o[...] = proj_t(wvf).astype(BF16)

    z = proj_t(wfz)[0:FOX_HEADS] + bf_ref[...]
    ls = jnp.minimum(z, 0.0) - jnp.log(1.0 + jnp.exp(-jnp.abs(z)))
    r = lax.broadcasted_iota(jnp.int32, (tm, tm), 0)
    c = lax.broadcasted_iota(jnp.int32, (tm, tm), 1)
    upto = jnp.where(r <= c, 1.0, 0.0).astype(BF16)
    csum = sum(jnp.dot(p, upto, preferred_element_type=F32) for p in _split3(ls))

    @pl.when(i % tiles_per_seq == 0)
    def _():
        carry_ref[...] = jnp.zeros_like(carry_ref)

    cum = csum + carry_ref[:, 0:1]
    carry_ref[...] = jnp.broadcast_to(cum[:, tm - 1:tm], carry_ref.shape)
    parts = [p.astype(F32) for p in _split3(-LOG2E * cum)]
    pad = jnp.zeros((LANES - BIAS_PARTS * FOX_HEADS, tm), F32)
    caux_o[...] = jnp.concatenate(parts + [pad], axis=0).T.astype(BF16)

    gw = gate_o.shape[1]
    step = 512
    for cb in range(gw // step):
        zz = jnp.dot(h, wgz[:, cb * step:(cb + 1) * step], preferred_element_type=F32)
        gate_o[:, cb * step:(cb + 1) * step] = jax.nn.sigmoid(zz).astype(BF16)


def _inproj(x2d, g1, wts, wgz, gains, bfz, cos_t, sin_t, seq, tm):
    t, d = x2d.shape
    row_spec = lambda w: pl.BlockSpec((tm, w), lambda i: (i, 0))
    col_spec = lambda r: pl.BlockSpec((r, tm), lambda i: (0, i))
    in_specs = ([row_spec(d), _const_spec(g1.shape)] + [_const_spec(w.shape) for w in wts] + [_const_spec(wgz.shape)]
                + [_const_spec(g.shape) for g in gains] + [_const_spec(bfz.shape), col_spec(cos_t.shape[0]),
                                                           col_spec(sin_t.shape[0])])
    fw = wts[0].shape[0]
    tr = jax.ShapeDtypeStruct((fw, t), BF16)
    rm = jax.ShapeDtypeStruct((t, fw), BF16)
    out_shape = [tr, rm, tr, tr, rm, tr, jax.ShapeDtypeStruct((t, LANES), BF16),
                 jax.ShapeDtypeStruct((t, wgz.shape[1]), BF16)]
    out_specs = [col_spec(fw), row_spec(fw), col_spec(fw), col_spec(fw), row_spec(fw), col_spec(fw),
                 row_spec(LANES), row_spec(wgz.shape[1])]
    return pl.pallas_call(
        functools.partial(_inproj_kernel, tiles_per_seq=seq // tm, tm=tm),
        grid=(t // tm,), in_specs=in_specs, out_specs=out_specs, out_shape=out_shape,
        scratch_shapes=[pltpu.VMEM((FOX_HEADS, LANES), F32)],
        compiler_params=_cparams(("arbitrary",)), name="inproj",
    )(x2d, g1, *wts, wgz, *gains, bfz, cos_t, sin_t)


DIFF_KEY_SPLIT = 2
FOX_KEY_SPLIT = 4
ONES_ROWS = 2 * SUBLANES


def _masked_scores(kx, w, masked):
    s = jnp.dot(kx, w, preferred_element_type=F32)
    if masked:
        key = lax.broadcasted_iota(jnp.int32, s.shape, 0)
        qry = lax.broadcasted_iota(jnp.int32, s.shape, 1)
        s = jnp.where(key <= qry, s, NEG_BIG)
    return s


def _absorb(s, v_ext, m, acc_ref, c, q_lo):
    nq = s.shape[1]
    m_new = jnp.maximum(m, jnp.max(s, axis=0, keepdims=True))
    alpha = jnp.exp2(m - m_new)
    p = jnp.exp2((s - m_new).astype(BF16))
    acc_ref[c, :, q_lo:q_lo + nq] = (alpha * acc_ref[c, :, q_lo:q_lo + nq]
                                     + jnp.dot(v_ext, p, preferred_element_type=F32))
    return m_new


def _causal_sweep(qi, n_chains, tq, key_split, keys, values, ws, acc_ref):
    half = tq // key_split
    chains = range(n_chains)

    def block(k0, ms, masked):
        subs = tuple((k0 + i * half, i * half if masked else 0) for i in range(key_split))
        scores = [[_masked_scores(keys(ks, half, c), ws[c][:, q_lo:], masked) for c in chains]
                  for ks, q_lo in subs]
        for (ks, q_lo), ss in zip(subs, scores):
            new = [_absorb(ss[c], values(ks, half, c), ms[c][:, q_lo:], acc_ref, c, q_lo) for c in chains]
            ms = tuple(jnp.concatenate([ms[c][:, :q_lo], new[c]], axis=1) if q_lo else new[c] for c in chains)
        return ms

    init = tuple(jnp.full((1, tq), NEG_BIG, F32) for _ in chains)
    ms = lax.fori_loop(0, qi, lambda j, ms: block(pl.multiple_of(j * tq, tq), ms, False), init)
    block(pl.multiple_of(qi * tq, tq), ms, True)


def _diff_attn_kernel(q1_ref, q2_ref, k1_ref, k2_ref, v_ref, lq1, lk1, lq2, lk2, sg_ref, o_ref, acc_ref,
                      *, tq, lambda_init):
    qi = pl.program_id(2)
    dv = 2 * HEAD_DIM
    zeros = jnp.zeros((HEAD_DIM, tq), BF16)
    ws = []
    for q_ref in (q1_ref, q2_ref):
        ws.append(jnp.concatenate([q_ref[0:HEAD_DIM, :], zeros], axis=0))
        ws.append(jnp.concatenate([zeros, q_ref[HEAD_DIM:2 * HEAD_DIM, :]], axis=0))
    acc_ref[...] = jnp.zeros_like(acc_ref)

    def keys(k0, tks, c):
        return (k1_ref, k1_ref, k2_ref, k2_ref)[c][pl.ds(pl.multiple_of(k0, tks), tks), :]

    def values(k0, tks, c):
        u = c % 2
        v = v_ref[u * dv:(u + 1) * dv, pl.ds(pl.multiple_of(k0, tks), tks)]
        return jnp.concatenate([v, jnp.ones((ONES_ROWS, tks), BF16)], axis=0)

    _causal_sweep(qi, 4, tq, DIFF_KEY_SPLIT, keys, values, ws, acc_ref)
    lam = (jnp.exp(jnp.sum(lq1[...] * lk1[...], axis=0, keepdims=True))
           - jnp.exp(jnp.sum(lq2[...] * lk2[...], axis=0, keepdims=True)) + lambda_init)
    sg = sg_ref[...] * (1.0 - lambda_init)
    outs = []
    for c1, c2 in ((0, 2), (1, 3)):
        o = (acc_ref[c1, 0:dv, :] / acc_ref[c1, dv:dv + 1, :]
             - lam * (acc_ref[c2, 0:dv, :] / acc_ref[c2, dv:dv + 1, :]))
        oms = jnp.mean(o * o, axis=0, keepdims=True)
        outs.append((o * lax.rsqrt(oms + EPS) * sg).T)
    o_ref[...] = jnp.concatenate(outs, axis=1).astype(o_ref.dtype)


def _diff_attention(qa_t, ka, va_t, lq1, lk1, lq2, lk2, subg, lambda_init, b, s, tq):
    t = b * s
    nq = s // tq
    pair = 2 * HEAD_DIM
    dv = 2 * HEAD_DIM
    n_pairs = DIFF_HEADS // 2
    q_spec = lambda off: pl.BlockSpec((pair, tq), lambda bi, hp, qi: (hp + off, bi * nq + qi))
    k_spec = lambda off: pl.BlockSpec((s, pair), lambda bi, hp, qi: (bi, hp + off))
    vec = lambda a: pl.BlockSpec(a.shape, lambda bi, hp, qi: (0, 0))
    return pl.pallas_call(
        functools.partial(_diff_attn_kernel, tq=tq, lambda_init=lambda_init),
        grid=(b, n_pairs, nq),
        in_specs=[q_spec(0), q_spec(n_pairs), k_spec(0), k_spec(n_pairs),
                  pl.BlockSpec((2 * dv, s), lambda bi, hp, qi: (hp, bi)),
                  vec(lq1), vec(lk1), vec(lq2), vec(lk2), vec(subg)],
        out_specs=pl.BlockSpec((tq, 2 * dv), lambda bi, hp, qi: (bi * nq + qi, hp)),
        out_shape=jax.ShapeDtypeStruct((t, DIFF_HEADS * dv), BF16),
        scratch_shapes=[pltpu.VMEM((4, dv + ONES_ROWS, tq), F32)],
        compiler_params=_cparams(("arbitrary", "arbitrary", "arbitrary")), name="diff_attn",
    )(qa_t, qa_t, ka, ka, va_t, lq1, lk1, lq2, lk2, subg)


def _fox_attn_kernel(q_ref, k_ref, caux_ref, v_ref, o_ref, acc_ref, *, tq):
    hp = pl.program_id(1)
    qi = pl.program_id(2)
    zeros = jnp.zeros((HEAD_DIM, tq), BF16)
    row = lax.broadcasted_iota(jnp.int32, (LANES, tq), 0)
    ws = []
    for u in range(2):
        head = 2 * hp + u
        pick = jnp.where(row < BIAS_PARTS * FOX_HEADS, jnp.where(row % FOX_HEADS == head, 1.0, 0.0), 0.0).astype(BF16)
        qh = q_ref[u * HEAD_DIM:(u + 1) * HEAD_DIM, :]
        ws.append(jnp.concatenate([qh, zeros, pick] if u == 0 else [zeros, qh, pick], axis=0))
    acc_ref[...] = jnp.zeros_like(acc_ref)

    def keys(k0, tks, c):
        ks = pl.ds(pl.multiple_of(k0, tks), tks)
        return jnp.concatenate([k_ref[ks, :], caux_ref[ks, :]], axis=1)

    def values(k0, tks, c):
        v = v_ref[c * HEAD_DIM:(c + 1) * HEAD_DIM, pl.ds(pl.multiple_of(k0, tks), tks)]
        return jnp.concatenate([v, jnp.ones((ONES_ROWS, tks), BF16)], axis=0)

    _causal_sweep(qi, 2, tq, FOX_KEY_SPLIT, keys, values, ws, acc_ref)
    o = jnp.concatenate([acc_ref[u, 0:HEAD_DIM, :] / acc_ref[u, HEAD_DIM:HEAD_DIM + 1, :] for u in range(2)],
                        axis=0)
    o_ref[...] = o.T.astype(o_ref.dtype)


def _fox_attention(qf_t, kf, caux, vf_t, b, s, tq):
    t = b * s
    nq = s // tq
    pair = 2 * HEAD_DIM
    return pl.pallas_call(
        functools.partial(_fox_attn_kernel, tq=tq),
        grid=(b, FOX_HEADS // 2, nq),
        in_specs=[pl.BlockSpec((pair, tq), lambda bi, hp, qi: (hp, bi * nq + qi)),
                  pl.BlockSpec((s, pair), lambda bi, hp, qi: (bi, hp)),
                  pl.BlockSpec((s, LANES), lambda bi, hp, qi: (bi, 0)),
                  pl.BlockSpec((pair, s), lambda bi, hp, qi: (hp, bi))],
        out_specs=pl.BlockSpec((tq, pair), lambda bi, hp, qi: (bi * nq + qi, hp)),
        out_shape=jax.ShapeDtypeStruct((t, FOX_HEADS * HEAD_DIM), BF16),
        scratch_shapes=[pltpu.VMEM((2, HEAD_DIM + ONES_ROWS, tq), F32)],
        compiler_params=_cparams(("arbitrary", "arbitrary", "arbitrary")), name="fox_attn",
    )(qf_t, kf, caux, vf_t)


def _tile_rows_shape(n, d):
    assert d == SUBLANES * LANES
    return (n * SUBLANES, LANES)


def _tile_row(ref, r):
    return ref.at[pl.ds(pl.multiple_of(r * SUBLANES, SUBLANES), SUBLANES), :]


def _store_tile_rows(ref, a):
    for j in range(SUBLANES):
        ref[pl.ds(j, a.shape[0], stride=SUBLANES), :] = a[:, j * LANES:(j + 1) * LANES]


def _load_tile_rows(ref, lo, n):
    return jnp.concatenate([ref[pl.ds(lo * SUBLANES + j, n, stride=SUBLANES), :] for j in range(SUBLANES)], axis=1)


ROUTE_ROWS = 8
EXPERT_ROW0 = 8


def _post_kernel(x_ref, oa_ref, of_ref, gate_ref, wda, wfo, wo, g2_ref, wrt_ref, brt_ref,
                 x_o, h2_o, route_o, cnt_o, cnt_ref, *, tm):
    i = pl.program_id(0)
    d = x_ref.shape[1]
    ya = jnp.dot(oa_ref[...], wda[...], preferred_element_type=F32)
    yf = jnp.dot(of_ref[...], wfo[...], preferred_element_type=F32)
    z = gate_ref[:, :d].astype(F32) * ya + gate_ref[:, d:].astype(F32) * yf
    x = x_ref[...] + jnp.dot(z.astype(BF16), wo[...], preferred_element_type=F32)
    x_o[...] = x
    ms = jnp.mean(x * x, axis=-1, keepdims=True)
    h2 = x * lax.rsqrt(ms + EPS) * g2_ref[...]
    _store_tile_rows(h2_o, h2)

    lg = lax.dot_general(wrt_ref[...], h2, (((1,), (1,)), ((), ())), preferred_element_type=F32,
                         precision=lax.Precision.HIGHEST) + brt_ref[...]
    sub = lax.broadcasted_iota(jnp.int32, (EXPERTS_PER_GROUP, tm), 0).astype(F32)
    gl = jnp.where(sub < N_GROUPS, lg[0:EXPERTS_PER_GROUP], NEG_BIG)
    gmax = jnp.max(gl, axis=0, keepdims=True)
    p_g = 1.0 / jnp.sum(jnp.exp(gl - gmax), axis=0, keepdims=True)
    g_idx = jnp.min(jnp.where(gl == gmax, sub, float(EXPERTS_PER_GROUP)), axis=0, keepdims=True)
    el = lg[EXPERT_ROW0:EXPERT_ROW0 + EXPERTS_PER_GROUP]
    for g in range(1, N_GROUPS):
        lo = EXPERT_ROW0 + EXPERTS_PER_GROUP * g
        el = jnp.where(g_idx == g, lg[lo:lo + EXPERTS_PER_GROUP], el)
    emax = jnp.max(el, axis=0, keepdims=True)
    esum = jnp.sum(jnp.exp(el - emax), axis=0, keepdims=True)
    i0 = jnp.min(jnp.where(el == emax, sub, float(EXPERTS_PER_GROUP)), axis=0, keepdims=True)
    el2 = jnp.where(sub == i0, NEG_BIG, el)
    e2max = jnp.max(el2, axis=0, keepdims=True)
    i1 = jnp.min(jnp.where(el2 == e2max, sub, float(EXPERTS_PER_GROUP)), axis=0, keepdims=True)
    p0 = 1.0 / esum
    p1 = jnp.exp(e2max - emax) / esum
    w0 = p_g * (p0 / (p0 + p1))
    w1 = p_g * (p1 / (p0 + p1))
    e0 = g_idx * EXPERTS_PER_GROUP + i0
    e1 = g_idx * EXPERTS_PER_GROUP + i1

    @pl.when(i == 0)
    def _():
        cnt_ref[...] = jnp.zeros_like(cnt_ref)

    erow = lax.broadcasted_iota(jnp.int32, (N_EXPERTS, tm), 0).astype(F32)
    oh0 = jnp.where(erow == e0, 1.0, 0.0)
    oh1 = jnp.where(erow == e1, 1.0, 0.0)
    both = oh0 + oh1
    r = lax.broadcasted_iota(jnp.int32, (tm, tm), 0)
    c = lax.broadcasted_iota(jnp.int32, (tm, tm), 1)
    earlier = jnp.where(r < c, 1.0, 0.0).astype(BF16)
    base = jnp.dot(both.astype(BF16), earlier, preferred_element_type=F32) + cnt_ref[:, 0:1]
    rank0 = jnp.sum(oh0 * base, axis=0, keepdims=True)
    rank1 = jnp.sum(oh1 * base, axis=0, keepdims=True)
    cnt_ref[...] = cnt_ref[...] + jnp.sum(both, axis=1, keepdims=True)
    cnt_o[...] = cnt_ref[...]

    route_o[...] = jnp.where(sub == 0, e0, jnp.where(sub == 1, e1, jnp.where(sub == 2, w0, jnp.where(
        sub == 3, w1, jnp.where(sub == 4, rank0, jnp.where(sub == 5, rank1, 0.0))))))


def _post(x2d, oa, of, gates, wda, wfo, wo, g2, wrt, brt, tm):
    t, d = x2d.shape
    row_spec = lambda w: pl.BlockSpec((tm, w), lambda i: (i, 0))
    return pl.pallas_call(
        functools.partial(_post_kernel, tm=tm),
        grid=(t // tm,),
        in_specs=[row_spec(d), row_spec(oa.shape[1]), row_spec(of.shape[1]), row_spec(gates.shape[1]),
                  _const_spec(wda.shape), _const_spec(wfo.shape), _const_spec(wo.shape), _const_spec(g2.shape),
                  _const_spec(wrt.shape), _const_spec(brt.shape)],
        out_specs=[row_spec(d), pl.BlockSpec(_tile_rows_shape(tm, d), lambda i: (i, 0)),
                   pl.BlockSpec((ROUTE_ROWS, tm), lambda i: (0, i)), _const_spec((N_EXPERTS, LANES))],
        out_shape=[jax.ShapeDtypeStruct((t, d), F32), jax.ShapeDtypeStruct(_tile_rows_shape(t, d), F32),
                   jax.ShapeDtypeStruct((ROUTE_ROWS, t), F32), jax.ShapeDtypeStruct((N_EXPERTS, LANES), F32)],
        scratch_shapes=[pltpu.VMEM((N_EXPERTS, LANES), F32)],
        compiler_params=_cparams(("arbitrary",)), name="post_attn",
    )(x2d, oa, of, gates, wda, wfo, wo, g2, wrt, brt)


def _dispatch_kernel(pos_ref, h2_ref, xs_in, xs_hbm, sem, *, tm):
    del xs_in

    def copy(r, k):
        return pltpu.make_async_copy(_tile_row(h2_ref, r), _tile_row(xs_hbm, pos_ref[0, 0, k * tm + r]), sem)

    def issue(r, c):
        copy(r, 0).start(priority=0)
        copy(r, 1).start(priority=1)
        return c
    lax.fori_loop(0, tm, issue, 0, unroll=8)

    def wait(r, c):
        copy(r, 0).wait()
        copy(r, 1).wait()
        return c
    lax.fori_loop(0, tm, wait, 0, unroll=8)


def _dispatch(pos3d, h2, xs_zero, tm):
    t = h2.shape[0] // SUBLANES
    return pl.pallas_call(
        functools.partial(_dispatch_kernel, tm=tm),
        grid=(t // tm,),
        in_specs=[pl.BlockSpec((1, 1, 2 * tm), lambda i: (i, 0, 0), memory_space=pltpu.SMEM),
                  pl.BlockSpec((tm * SUBLANES, LANES), lambda i: (i, 0)),
                  pl.BlockSpec(memory_space=pl.ANY)],
        out_specs=pl.BlockSpec(memory_space=pl.ANY),
        out_shape=jax.ShapeDtypeStruct(xs_zero.shape, xs_zero.dtype),
        scratch_shapes=[pltpu.SemaphoreType.DMA(())],
        input_output_aliases={2: 0},
        compiler_params=_cparams(("arbitrary",)), name="moe_dispatch",
    )(pos3d, h2, xs_zero)


def _moe_kernel(te_ref, tv_ref, x_ref, wg_ref, wu_ref, wd_ref, y_ref):
    i = pl.program_id(0)

    @pl.when(tv_ref[i] == 1)
    def _():
        x = _load_tile_rows(x_ref, 0, x_ref.shape[0] // SUBLANES).astype(BF16)
        a = jnp.dot(x, wg_ref[0].astype(BF16), preferred_element_type=F32)
        u = jnp.dot(x, wu_ref[0].astype(BF16), preferred_element_type=F32)
        hid = (a * jax.nn.sigmoid(a) * u).astype(BF16)
        _store_tile_rows(y_ref, jnp.dot(hid, wd_ref[0].astype(BF16), preferred_element_type=F32))

    @pl.when(tv_ref[i] == 0)
    def _():
        y_ref[...] = jnp.zeros_like(y_ref)


def _moe(tile_expert, tile_valid, xs, wg, wu, wd, tm):
    n_tiles = xs.shape[0] // (tm * SUBLANES)
    d, f = wg.shape[1], wg.shape[2]
    rows_spec = pl.BlockSpec(_tile_rows_shape(tm, d), lambda i, te, tv: (i, 0))
    grid_spec = pltpu.PrefetchScalarGridSpec(
        num_scalar_prefetch=2, grid=(n_tiles,),
        in_specs=[rows_spec,
                  pl.BlockSpec((1, d, f), lambda i, te, tv: (te[i], 0, 0)),
                  pl.BlockSpec((1, d, f), lambda i, te, tv: (te[i], 0, 0)),
                  pl.BlockSpec((1, f, d), lambda i, te, tv: (te[i], 0, 0))],
        out_specs=rows_spec)
    return pl.pallas_call(
        _moe_kernel, grid_spec=grid_spec,
        out_shape=jax.ShapeDtypeStruct(_tile_rows_shape(n_tiles * tm, d), F32),
        compiler_params=_cparams(("arbitrary",)), name="moe_experts",
    )(tile_expert, tile_valid, xs, wg, wu, wd)


def _combine_kernel(pos_ref, x_ref, route_ref, y_hbm, o_ref, ybuf, sem, *, tm):
    def copy(r):
        return pltpu.make_async_copy(_tile_row(y_hbm, pos_ref[0, 0, r]), _tile_row(ybuf, r), sem)

    def issue(r, c):
        copy(r).start(priority=0)
        copy(tm + r).start(priority=1)
        return c
    lax.fori_loop(0, tm, issue, 0, unroll=8)

    def wait(r, c):
        copy(r).wait()
        return c
    lax.fori_loop(0, 2 * tm, wait, 0, unroll=8)

    d = x_ref.shape[1]
    w0 = jnp.tile(jnp.broadcast_to(route_ref[2:3, :], (LANES, tm)).T, (1, d // LANES))
    w1 = jnp.tile(jnp.broadcast_to(route_ref[3:4, :], (LANES, tm)).T, (1, d // LANES))
    o_ref[...] = x_ref[...] + w0 * _load_tile_rows(ybuf, 0, tm) + w1 * _load_tile_rows(ybuf, tm, tm)


def _combine(pos3d, x2d, route, y, tm):
    t, d = x2d.shape
    return pl.pallas_call(
        functools.partial(_combine_kernel, tm=tm),
        grid=(t // tm,),
        in_specs=[pl.BlockSpec((1, 1, 2 * tm), lambda i: (i, 0, 0), memory_space=pltpu.SMEM),
                  pl.BlockSpec((tm, d), lambda i: (i, 0)),
                  pl.BlockSpec((ROUTE_ROWS, tm), lambda i: (0, i)),
                  pl.BlockSpec(memory_space=pl.ANY)],
        out_specs=pl.BlockSpec((tm, d), lambda i: (i, 0)),
        out_shape=jax.ShapeDtypeStruct((t, d), F32),
        scratch_shapes=[pltpu.VMEM(_tile_rows_shape(2 * tm, d), F32), pltpu.SemaphoreType.DMA(())],
        compiler_params=_cparams(("arbitrary",)), name="moe_combine",
    )(pos3d, x2d, route, y)


def _dispatch_plan(route, counts, tm_e, tm_c):
    t = route.shape[1]
    counts = counts.astype(jnp.int32)
    padded = ((counts + tm_e - 1) // tm_e) * tm_e
    pad_end = jnp.cumsum(padded)
    pad_off = pad_end - padded
    n_tiles = (2 * t) // tm_e + N_EXPERTS
    tile_row0 = jnp.arange(n_tiles, dtype=jnp.int32) * tm_e
    tile_valid = (tile_row0 < pad_end[-1]).astype(jnp.int32)
    tile_expert = jnp.minimum(jnp.sum((tile_row0[:, None] >= pad_end[None, :]).astype(jnp.int32), axis=1),
                              N_EXPERTS - 1).astype(jnp.int32)
    e_ids = route[0:2].astype(jnp.int32)
    ranks = route[4:6].astype(jnp.int32)
    onehot = e_ids[:, :, None] == jnp.arange(N_EXPERTS, dtype=jnp.int32)[None, None, :]
    pos = ranks + jnp.sum(jnp.where(onehot, pad_off[None, None, :], 0), axis=-1)
    pos3d = pos.reshape(2, t // tm_c, tm_c).transpose(1, 0, 2).reshape(t // tm_c, 1, 2 * tm_c)
    return tile_expert, tile_valid, pos3d, n_tiles


def _rotary_tables(positions):
    inv_freq = ROPE_THETA ** (-jnp.arange(0, ROT_DIM, 2, dtype=F32) / ROT_DIM)
    ang = inv_freq[:, None] * positions.astype(F32).reshape(-1)[None, :]
    return jnp.cos(ang), jnp.sin(ang)


def kernel(x, positions, ln1_g, w_in, b_forget, q_norm_diff_g, k_norm_diff_g, lam_q1, lam_k1, lam_q2, lam_k2,
           subln_g, q_norm_fox_g, k_norm_fox_g, w_diff_out, w_fox_out, w_o, ln2_g, w_router_group,
           b_router_group, w_router_expert, b_router_expert, w_exp_gate, w_exp_up, w_exp_down):
    b, s, d = x.shape
    depth = w_in.shape[0]
    t = b * s
    tm = min(512, s)
    tq = min(512, s)
    tq_fox = min(1024, s)
    tm_e = 256
    tm_c = min(512, s)
    assert s % tm == 0 and s % tq == 0 and s % tq_fox == 0 and t % tm_c == 0 and (2 * t) % tm_e == 0

    qk_w = 2 * DIFF_HEADS * HEAD_DIM
    v_w = DIFF_HEADS * 2 * HEAD_DIM
    f_w = FOX_HEADS * HEAD_DIM
    assert qk_w == v_w == f_w
    bounds = [0]
    for w in (qk_w, qk_w, v_w, f_w, f_w, f_w, FOX_HEADS, 2 * d):
        bounds.append(bounds[-1] + w)

    cos_t, sin_t = _rotary_tables(positions)
    col = lambda g: g.astype(F32)[:, None]

    x2d = x.reshape(t, d)
    for l in range(depth):
        lambda_init = 0.8 - 0.6 * math.exp(-0.3 * l)
        wl = w_in[l]
        wts = [wl[:, bounds[k]:bounds[k + 1]].T.astype(BF16) for k in range(7)]
        wts[6] = jnp.pad(wts[6], ((0, 2 * SUBLANES - FOX_HEADS), (0, 0)))
        wgz = wl[:, bounds[7]:bounds[8]].astype(BF16)
        gains = [col(q_norm_diff_g[l]), col(k_norm_diff_g[l]), col(q_norm_fox_g[l]), col(k_norm_fox_g[l])]
        qa_t, ka, va_t, qf_t, kf, vf_t, caux, gates = _inproj(
            x2d, ln1_g[l][None, :], wts, wgz, gains, col(b_forget[l]), cos_t, sin_t, s, tm)

        oa = _diff_attention(qa_t, ka, va_t, col(lam_q1[l]), col(lam_k1[l]), col(lam_q2[l]), col(lam_k2[l]),
                             col(subln_g[l]), lambda_init, b, s, tq)
        of = _fox_attention(qf_t, kf, caux, vf_t, b, s, tq_fox)

        wrt = jnp.zeros((LANES, d), F32)
        wrt = wrt.at[:N_GROUPS].set(w_router_group[l].T.astype(F32))
        wrt = wrt.at[EXPERT_ROW0:EXPERT_ROW0 + N_EXPERTS].set(w_router_expert[l].T.astype(F32))
        brt = jnp.zeros((LANES, 1), F32)
        brt = brt.at[:N_GROUPS, 0].set(b_router_group[l].astype(F32))
        brt = brt.at[EXPERT_ROW0:EXPERT_ROW0 + N_EXPERTS, 0].set(b_router_expert[l].astype(F32))
        x_mid, h2, route, counts = _post(x2d, oa, of, gates,
                                         w_diff_out[l].astype(BF16), w_fox_out[l].astype(BF16),
                                         w_o[l].astype(BF16), ln2_g[l][None, :], wrt, brt, tm)

        tile_expert, tile_valid, pos3d, n_tiles = _dispatch_plan(route, counts[:, 0], tm_e, tm_c)
        xs = _dispatch(pos3d, h2, jnp.zeros(_tile_rows_shape(n_tiles * tm_e, d), F32), tm_c)
        f = w_exp_gate.shape[-1]
        y = _moe(tile_expert + l * N_EXPERTS, tile_valid, xs,
                 w_exp_gate.reshape(depth * N_EXPERTS, d, f), w_exp_up.reshape(depth * N_EXPERTS, d, f),
                 w_exp_down.reshape(depth * N_EXPERTS, f, d), tm_e)
        x2d = _combine(pos3d, x_mid, route, y, tm_c)
    return x2d.reshape(b, s, d)
```
